```python
import math
import jax, jax.numpy as jnp
from jax import lax
import numpy as np

D_MODEL = 2048
BATCH = 16
SEQ = 256
DEPTH = 2
DEC_BATCH = 2
DEC_SEQ = 4096
PAST_LEN = 256

GRID_W = 64
D_MIX = 2 * D_MODEL
SSD_WIDTH = D_MIX // 2
SSD_HEAD_DIM = 64
SSD_HEADS = SSD_WIDTH // SSD_HEAD_DIM
SSD_GROUPS = 4
SSD_STATE = 128
SSD_CHUNK = 128
CONV_K = 5
CONV_DIM = SSD_WIDTH + 2 * SSD_GROUPS * SSD_STATE
FNET_WIDTH = D_MIX // 4
FNET_GROUPS = 4
SGU_WIDTH = D_MIX // 4
SGU_HEADS = 8
SGU_HEAD_DIM = SGU_WIDTH // SGU_HEADS
SGU_CHUNK = 128
D_FF = 5632
N_MOD = 9
EPS = 1e-6
OFF_XBC = SSD_WIDTH
OFF_DT = OFF_XBC + CONV_DIM
OFF_FNET = OFF_DT + 2 * SSD_HEADS
OFF_SGU = OFF_FNET + FNET_WIDTH
IN_COLS = OFF_SGU + 2 * SGU_WIDTH

kernel_name = "hybrid_ssd_fnet_sgu_diffusion_step"


def rms_norm(x, g):
    xf = x.astype(jnp.float32)
    xf = xf * lax.rsqrt(jnp.mean(xf * xf, axis=-1, keepdims=True) + EPS)
    return (xf * g.astype(jnp.float32)).astype(x.dtype)


def swiglu(h, w_up, w_down):
    g, u = jnp.split(h @ w_up, 2, axis=-1)
    return (jax.nn.silu(g) * u) @ w_down


def depthwise_conv(u, w, b, grid_rows):
    bsz, L, C = u.shape
    if grid_rows:
        rows = L // GRID_W
        u = u.reshape(bsz * rows, GRID_W, C)
    out = lax.conv_general_dilated(
        u, w[:, None, :].astype(u.dtype), window_strides=(1,),
        padding=[(CONV_K // 2, CONV_K // 2)],
        dimension_numbers=('NWC', 'WIO', 'NWC'), feature_group_count=C)
    return out.reshape(bsz, L, C) + b


def segsum_exp(a_cs):
    T = a_cs.shape[-1]
    diff = a_cs[..., :, None] - a_cs[..., None, :]
    mask = jnp.tril(jnp.ones((T, T), dtype=bool))
    return jnp.exp(jnp.where(mask, diff, -jnp.inf))


def ssd_scan(xs, dt, a_neg, bmat, cmat, h0):
    b, l, H, P = xs.shape
    G, N = bmat.shape[2], bmat.shape[3]
    R = H // G
    T = SSD_CHUNK
    nc = l // T
    x = (xs * dt[..., None]).reshape(b, nc, T, G, R, P)
    a = jnp.moveaxis((dt * a_neg).reshape(b, nc, T, G, R), 2, -1)
    a_cs = jnp.cumsum(a, axis=-1)
    Bc = bmat.reshape(b, nc, T, G, N)
    Cc = cmat.reshape(b, nc, T, G, N)
    Lm = segsum_exp(a_cs)
    cb = jnp.einsum('bctgn,bcsgn->bcgts', Cc, Bc)
    y_diag = jnp.einsum('bcgts,bcgrts,bcsgrp->bctgrp', cb, Lm, x)
    decay_to_end = jnp.exp(a_cs[..., -1:] - a_cs)
    states = jnp.einsum('bctgn,bcgrt,bctgrp->bcgrpn', Bc, decay_to_end, x)
    chunk_decay = jnp.exp(a_cs[..., -1])

    def step(h, inp):
        s, d = inp
        return h * d[..., None, None] + s, h

    h_final, h_prev = lax.scan(step, h0.reshape(b, G, R, P, N),
                               (jnp.moveaxis(states, 1, 0), jnp.moveaxis(chunk_decay, 1, 0)))
    h_prev = jnp.moveaxis(h_prev, 0, 1)
    y_off = jnp.einsum('bctgn,bcgrpn,bcgrt->bctgrp', Cc, h_prev, jnp.exp(a_cs))
    y = (y_diag + y_off).reshape(b, l, H, P)
    return y, h_final.reshape(b, H, P, N)


def token_mixer(h, w_in, conv_w, conv_b, a_log, dt_bias, d_skip, g_ssd, g_sgu, w_sp, b_sp, w_out,
                h0, grid_rows):
    bsz, L, _ = h.shape
    f32 = jnp.float32
    proj = h @ w_in
    z = proj[..., :SSD_WIDTH]
    xbc = proj[..., OFF_XBC:OFF_DT]
    dt_raw = proj[..., OFF_DT:OFF_FNET]
    f_in = proj[..., OFF_FNET:OFF_SGU]
    s_in = proj[..., OFF_SGU:]

    xbc = jax.nn.silu(depthwise_conv(xbc, conv_w, conv_b, grid_rows))
    gn = SSD_GROUPS * SSD_STATE
    xs = xbc[..., :SSD_WIDTH].reshape(bsz, L, SSD_HEADS, SSD_HEAD_DIM).astype(f32)
    bm = xbc[..., SSD_WIDTH:SSD_WIDTH + gn].reshape(bsz, L, SSD_GROUPS, SSD_STATE).astype(f32)
    cm = xbc[..., SSD_WIDTH + gn:].reshape(bsz, L, SSD_GROUPS, SSD_STATE).astype(f32)
    dt = jax.nn.softplus(dt_raw.astype(f32).reshape(bsz, L, 2, SSD_HEADS) + dt_bias.astype(f32))
    a_neg = -jnp.exp(a_log.astype(f32))
    h0f = h0.astype(f32)
    y_f, hf_f = ssd_scan(xs, dt[:, :, 0], a_neg[0], bm, cm, h0f[:, 0])
    y_b, hf_b = ssd_scan(xs[:, ::-1], dt[:, ::-1, 1], a_neg[1], bm[:, ::-1], cm[:, ::-1], h0f[:, 1])
    d_sum = (d_skip[0] + d_skip[1]).astype(f32)
    y = y_f + y_b[:, ::-1] + xs * d_sum[:, None]
    y = y.reshape(bsz, L, SSD_WIDTH).astype(h.dtype)
    y_ssd = rms_norm(y * jax.nn.silu(z), g_ssd)
    new_h = jnp.stack([hf_f, hf_b], axis=1).astype(h.dtype)

    f = f_in.reshape(bsz, L, FNET_GROUPS, FNET_WIDTH // FNET_GROUPS).astype(f32)
    y_fnet = jnp.real(jnp.fft.fft2(f, axes=(1, 3), norm='ortho'))
    y_fnet = y_fnet.reshape(bsz, L, FNET_WIDTH).astype(h.dtype)

    u, v = jnp.split(jax.nn.gelu(s_in), 2, axis=-1)
    v = rms_norm(v, g_sgu).reshape(bsz, L // SGU_CHUNK, SGU_CHUNK, SGU_HEADS, SGU_HEAD_DIM)
    sp = jnp.einsum('hts,bkshd->bkthd', w_sp, v) + b_sp.T[None, None, :, :, None]
    y_sgu = u * sp.reshape(bsz, L, SGU_WIDTH)

    out = jnp.concatenate([y_ssd, y_fnet, y_sgu], axis=-1) @ w_out
    return out, new_h


def trunk_layer(x, mod, h0, grid_rows, norm_g, w_ffn1_up, w_ffn1_down, w_in, conv_w, conv_b,
                a_log, dt_bias, d_skip, g_ssd, g_sgu, w_sp, b_sp, w_out, w_ffn2_up, w_ffn2_down):
    sh1, sc1, g1, sh2, sc2, g2, sh3, sc3, g3 = jnp.split(mod, N_MOD, axis=-1)
    h = rms_norm(x, norm_g[0]) * (1 + sc1) + sh1
    x = x + 0.5 * g1 * rms_norm(swiglu(h, w_ffn1_up, w_ffn1_down), norm_g[1])
    h = rms_norm(x, norm_g[2]) * (1 + sc2) + sh2
    m, new_h = token_mixer(h, w_in, conv_w, conv_b, a_log, dt_bias, d_skip, g_ssd, g_sgu,
                           w_sp, b_sp, w_out, h0, grid_rows)
    x = x + g2 * rms_norm(m, norm_g[3])
    h = rms_norm(x, norm_g[4]) * (1 + sc3) + sh3
    x = x + 0.5 * g3 * rms_norm(swiglu(h, w_ffn2_up, w_ffn2_down), norm_g[5])
    return x, new_h


def setup_inputs(seed: int = 0) -> dict:
    key = jax.random.key(seed)
    ks = jax.random.split(key, 26)
    f32 = jnp.float32

    def nrm(k, shape, s):
        return jax.random.normal(k, shape, f32) * s

    dt0 = jnp.exp(jax.random.uniform(ks[12], (DEPTH, 2, SSD_HEADS), f32,
                                     math.log(1e-3), math.log(1e-1)))
    dt_bias = dt0 + jnp.log(-jnp.expm1(-dt0))
    return {
        "x_prompt": nrm(ks[0], (BATCH, SEQ, D_MODEL), 1.0),
        "x_sample": nrm(ks[1], (DEC_BATCH, DEC_SEQ, D_MODEL), 1.0),
        "state_ssd": nrm(ks[2], (DEC_BATCH, DEPTH, 2, SSD_HEADS, SSD_HEAD_DIM, SSD_STATE), 0.1),
        "c": nrm(ks[3], (DEC_BATCH, D_MODEL), 1.0),
        "c_ctx": nrm(ks[4], (D_MODEL,), 1.0),
        "w_mod": nrm(ks[5], (DEPTH, D_MODEL, N_MOD * D_MODEL), 0.5 * D_MODEL ** -0.5),
        "b_mod": nrm(ks[6], (DEPTH, N_MOD * D_MODEL), 0.01),
        "norm_g": 1.0 + nrm(ks[7], (DEPTH, 6, D_MODEL), 0.02),
        "w_ffn1_up": nrm(ks[8], (DEPTH, D_MODEL, 2 * D_FF), D_MODEL ** -0.5),
        "w_ffn1_down": nrm(ks[9], (DEPTH, D_FF, D_MODEL), D_FF ** -0.5),
        "w_in": nrm(ks[10], (DEPTH, D_MODEL, IN_COLS), D_MODEL ** -0.5),
        "conv_w": nrm(ks[11], (DEPTH, CONV_K, CONV_DIM), CONV_K ** -0.5),
        "conv_b": nrm(ks[13], (DEPTH, CONV_DIM), 0.01),
        "a_log": jnp.log(jax.random.uniform(ks[14], (DEPTH, 2, SSD_HEADS), f32, 1.0, 16.0)),
        "dt_bias": dt_bias,
        "d_skip": 1.0 + nrm(ks[15], (DEPTH, 2, SSD_HEADS), 0.1),
        "g_ssd": 1.0 + nrm(ks[16], (DEPTH, SSD_WIDTH), 0.02),
        "g_sgu": 1.0 + nrm(ks[17], (DEPTH, SGU_WIDTH), 0.02),
        "w_sp": nrm(ks[18], (DEPTH, SGU_HEADS, SGU_CHUNK, SGU_CHUNK), SGU_CHUNK ** -0.5),
        "b_sp": nrm(ks[19], (DEPTH, SGU_HEADS, SGU_CHUNK), 0.01),
        "w_out": nrm(ks[20], (DEPTH, D_MIX, D_MODEL), D_MIX ** -0.5),
        "w_ffn2_up": nrm(ks[21], (DEPTH, D_MODEL, 2 * D_FF), D_MODEL ** -0.5),
        "w_ffn2_down": nrm(ks[22], (DEPTH, D_FF, D_MODEL), D_FF ** -0.5),
    }


def reference(x_prompt, x_sample, state_ssd, c, c_ctx, w_mod, b_mod, norm_g, w_ffn1_up,
              w_ffn1_down, w_in, conv_w, conv_b, a_log, dt_bias, d_skip, g_ssd, g_sgu, w_sp,
              b_sp, w_out, w_ffn2_up, w_ffn2_down):
    y_prompt = x_prompt
    y_sample = x_sample
    h0_ctx = jnp.zeros((x_prompt.shape[0], 2, SSD_HEADS, SSD_HEAD_DIM, SSD_STATE), x_prompt.dtype)
    ctx_states = []
    for l in range(DEPTH):
        layer_params = (norm_g[l], w_ffn1_up[l], w_ffn1_down[l], w_in[l], conv_w[l], conv_b[l],
                        a_log[l], dt_bias[l], d_skip[l], g_ssd[l], g_sgu[l], w_sp[l], b_sp[l],
                        w_out[l], w_ffn2_up[l], w_ffn2_down[l])
        mod_ctx = (jax.nn.silu(c_ctx) @ w_mod[l] + b_mod[l])[None, None, :]
        y_prompt, h_ctx = trunk_layer(y_prompt, mod_ctx, h0_ctx, False, *layer_params)
        ctx_states.append(h_ctx)
        mod_lat = (jax.nn.silu(c) @ w_mod[l] + b_mod[l])[:, None, :]
        y_sample, _ = trunk_layer(y_sample, mod_lat, state_ssd[:, l], True, *layer_params)
    new_state_ssd = jnp.stack(ctx_states, axis=1)
    return (y_prompt, y_sample, new_state_ssd)
```

```python
import functools
import math

import numpy as np
import jax
import jax.numpy as jnp
from jax import lax
from jax.experimental import pallas as pl
from jax.experimental.pallas import tpu as pltpu

F32 = jnp.float32
BF16 = jnp.bfloat16
EPS = 1e-6

LANES = 128
VMEM_LIMIT = 56 * 1024 * 1024

SSD_HEAD_DIM = 64
SSD_GROUPS = 4
SSD_STATE = 128
CHUNK = 128
CONV_K = 5
GRID_W = 64
FNET_GROUPS = 4
SGU_HEADS = 8
N_MOD = 9


def _cparams(sem):
    return pltpu.CompilerParams(dimension_semantics=sem, vmem_limit_bytes=VMEM_LIMIT)


def _rms(x, g):
    ms = jnp.mean(x * x, axis=-1, keepdims=True)
    return x * lax.rsqrt(ms + EPS) * g


def _silu(x):
    return x * (1.0 / (1.0 + jnp.exp(-x)))


def _split3(v):
    hi = v.astype(BF16)
    r1 = v - hi.astype(F32)
    mid = r1.astype(BF16)
    lo = (r1 - mid.astype(F32)).astype(BF16)
    return hi, mid, lo


def _dot(a, b):
    return jnp.dot(a, b, preferred_element_type=F32)


def _dot_exact_rhs(v, m_bf):
    hi, mid, lo = _split3(v)
    return _dot(hi, m_bf) + _dot(mid, m_bf) + _dot(lo, m_bf)


def _dot_exact_lhs(m_bf, v):
    hi, mid, lo = _split3(v)
    return _dot(m_bf, hi) + _dot(m_bf, mid) + _dot(m_bf, lo)


def _mod_kernel(c_ref, w_ref, b_ref, o_ref):
    s = _silu(c_ref[...]).astype(BF16)
    o_ref[0] = _dot(s, w_ref[0].astype(BF16)) + b_ref[0]


def _mod_call(cvec, w_mod, b_mod):
    depth, d, n = w_mod.shape
    tn = 1024
    return pl.pallas_call(
        _mod_kernel,
        grid=(depth, n // tn),
        in_specs=[
            pl.BlockSpec((8, d), lambda l, j: (0, 0)),
            pl.BlockSpec((1, d, tn), lambda l, j: (l, 0, j)),
            pl.BlockSpec((1, 1, tn), lambda l, j: (l, 0, j)),
        ],
        out_specs=pl.BlockSpec((1, 8, tn), lambda l, j: (l, 0, j)),
        out_shape=jax.ShapeDtypeStruct((depth, 8, n), F32),
        compiler_params=_cparams(("arbitrary", "arbitrary")),
        name="mod",
    )(cvec, w_mod, b_mod.reshape(depth, 1, n))


def _prenorm_kernel(x_ref, mod_ref, ng_ref, h_ref):
    h = _rms(x_ref[...], ng_ref[0:1, :]) * (1.0 + mod_ref[0, 1:2, :]) + mod_ref[0, 0:1, :]
    h_ref[...] = h.astype(BF16)


def _prenorm_call(x, mod, ng, rows_per_group):
    m, d = x.shape
    tm = 512
    return pl.pallas_call(
        _prenorm_kernel,
        grid=(m // tm,),
        in_specs=[
            pl.BlockSpec((tm, d), lambda i: (i, 0)),
            pl.BlockSpec((1, N_MOD, d), lambda i: (i * tm // rows_per_group, 0, 0)),
            pl.BlockSpec(ng.shape, lambda i: (0, 0)),
        ],
        out_specs=pl.BlockSpec((tm, d), lambda i: (i, 0)),
        out_shape=jax.ShapeDtypeStruct((m, d), BF16),
        compiler_params=_cparams(("arbitrary",)),
        name="prenorm",
    )(x, mod, ng)


def _ffn_kernel(h_ref, x_ref, wg_ref, wu_ref, wd_ref, mod_ref, ng_ref, modn_ref, ngn_ref,
                *rest, gate, ng_out, ng_next, sh_next, sc_next, emit_next):
    if emit_next:
        xo_ref, ho_ref, acc_ref = rest
    else:
        xo_ref, acc_ref = rest
    j = pl.program_id(1)
    h = h_ref[...]
    g = _dot(h, wg_ref[...])
    u = _dot(h, wu_ref[...])
    a = (_silu(g) * u).astype(BF16)
    d = _dot(a, wd_ref[...])

    @pl.when(j == 0)
    def _():
        acc_ref[...] = d

    @pl.when(j > 0)
    def _():
        acc_ref[...] += d

    @pl.when(j == pl.num_programs(1) - 1)
    def _():
        y = acc_ref[...]
        xn = x_ref[...] + 0.5 * mod_ref[0, gate:gate + 1, :] * _rms(y, ng_ref[ng_out:ng_out + 1, :])
        xo_ref[...] = xn
        if emit_next:
            hn = (_rms(xn, ngn_ref[ng_next:ng_next + 1, :])
                  * (1.0 + modn_ref[0, sc_next:sc_next + 1, :]) + modn_ref[0, sh_next:sh_next + 1, :])
            ho_ref[...] = hn.astype(BF16)


def _ffn_call(h, x, w_up, w_down, mod, ng, mod_next, ng_next_arr, rows_per_group, *, gate, ng_out,
              ng_next, sh_next, sc_next, emit_next):
    m, d = x.shape
    f = w_down.shape[0]
    tm, fc = 512, 512
    nf = f // fc
    kern = functools.partial(_ffn_kernel, gate=gate, ng_out=ng_out, ng_next=ng_next,
                             sh_next=sh_next, sc_next=sc_next, emit_next=emit_next)
    row = pl.BlockSpec((tm, d), lambda i, j: (i, 0))
    modspec = pl.BlockSpec((1, N_MOD, d), lambda i, j: (i * tm // rows_per_group, 0, 0))
    ngspec = pl.BlockSpec(ng.shape, lambda i, j: (0, 0))
    out_specs = [row]
    out_shape = [jax.ShapeDtypeStruct((m, d), F32)]
    if emit_next:
        out_specs.append(row)
        out_shape.append(jax.ShapeDtypeStruct((m, d), BF16))
    return pl.pallas_call(
        kern,
        grid=(m // tm, nf),
        in_specs=[
            row, row,
            pl.BlockSpec((d, fc), lambda i, j: (0, j)),
            pl.BlockSpec((d, fc), lambda i, j: (0, j + nf)),
            pl.BlockSpec((fc, d), lambda i, j: (j, 0)),
            modspec, ngspec, modspec, ngspec,
        ],
        out_specs=out_specs,
        out_shape=out_shape,
        scratch_shapes=[pltpu.VMEM((tm, d), F32)],
        compiler_params=_cparams(("arbitrary", "arbitrary")),
        name="ffn",
    )(h, x, w_up, w_up, w_down, mod, ng, mod_next, ng_next_arr)


def _z_kernel(h_ref, w_ref, o_ref):
    o_ref[...] = _silu(_dot(h_ref[...], w_ref[...])).astype(BF16)


def _z_call(h, w):
    m, d = h.shape
    n = w.shape[1]
    tm, tn = 1024, 1024
    return pl.pallas_call(
        _z_kernel,
        grid=(m // tm, n // tn),
        in_specs=[pl.BlockSpec((tm, d), lambda i, j: (i, 0)),
                  pl.BlockSpec((d, tn), lambda i, j: (0, j))],
        out_specs=pl.BlockSpec((tm, tn), lambda i, j: (i, j)),
        out_shape=jax.ShapeDtypeStruct((m, n), BF16),
        compiler_params=_cparams(("arbitrary", "arbitrary")),
        name="inproj_z",
    )(h, w)


def _xbc_kernel(h_ref, w_ref, cw_ref, cb_ref, o_ref, *, ctx_rows, ctx_seg, lat_seg):
    tm, tn = o_ref.shape
    i = pl.program_id(0)
    acc = _dot(h_ref[...], w_ref[...])
    seg = jnp.where(i * tm < ctx_rows, ctx_seg, lat_seg)
    pos = lax.broadcasted_iota(jnp.int32, (tm, LANES), 0) & (seg - 1)
    half = CONV_K // 2
    valid = {dlt: (pos + dlt >= 0) & (pos + dlt < seg) for dlt in range(-half, half + 1) if dlt != 0}
    for cblk in range(tn // LANES):
        sl = slice(cblk * LANES, (cblk + 1) * LANES)
        u = acc[:, sl]
        out = u * cw_ref[half:half + 1, sl] + cb_ref[0:1, sl]
        for dlt in range(-half, half + 1):
            if dlt == 0:
                continue
            shifted = pltpu.roll(u, (-dlt) % tm, 0)
            out = out + jnp.where(valid[dlt], shifted, 0.0) * cw_ref[half + dlt:half + dlt + 1, sl]
        o_ref[:, sl] = _silu(out).astype(BF16)


def _xbc_call(h, w, conv_w, conv_b, ctx_rows, ctx_seg):
    m, d = h.shape
    n = w.shape[1]
    tm, tn = 512, 1024
    kern = functools.partial(_xbc_kernel, ctx_rows=ctx_rows, ctx_seg=ctx_seg, lat_seg=GRID_W)
    return pl.pallas_call(
        kern,
        grid=(m // tm, n // tn),
        in_specs=[pl.BlockSpec((tm, d), lambda i, j: (i, 0)),
                  pl.BlockSpec((d, tn), lambda i, j: (0, j)),
                  pl.BlockSpec((CONV_K, tn), lambda i, j: (0, j)),
                  pl.BlockSpec((1, tn), lambda i, j: (0, j))],
        out_specs=pl.BlockSpec((tm, tn), lambda i, j: (i, j)),
        out_shape=jax.ShapeDtypeStruct((m, n), BF16),
        compiler_params=_cparams(("arbitrary", "arbitrary")),
        name="inproj_xbc",
    )(h, w, conv_w, conv_b.reshape(1, n))


def _dt_kernel(h_ref, w_ref, b_ref, o_ref):
    v = _dot(h_ref[...], w_ref[...]) + b_ref[...]
    o_ref[...] = jnp.maximum(v, 0.0) + jnp.log1p(jnp.exp(-jnp.abs(v)))


def _dt_call(h, w, b):
    m, d = h.shape
    n = w.shape[1]
    tm = 1024
    return pl.pallas_call(
        _dt_kernel,
        grid=(m // tm,),
        in_specs=[pl.BlockSpec((tm, d), lambda i: (i, 0)),
                  pl.BlockSpec((d, n), lambda i: (0, 0)),
                  pl.BlockSpec((1, n), lambda i: (0, 0))],
        out_specs=pl.BlockSpec((tm, n), lambda i: (i, 0)),
        out_shape=jax.ShapeDtypeStruct((m, n), F32),
        compiler_params=_cparams(("arbitrary",)),
        name="inproj_dt",
    )(h, w, b)


def _fnet_in_kernel(h_ref, w_ref, cs_ref, xc_ref, xs_ref):
    f = _dot(h_ref[...], w_ref[...]).astype(BF16)
    gw = cs_ref.shape[0]
    for g in range(FNET_GROUPS):
        sl = slice(g * gw, (g + 1) * gw)
        r = _dot(f[:, sl], cs_ref[...])
        xc_ref[:, sl] = r[:, :gw].astype(BF16)
        xs_ref[:, sl] = r[:, gw:].astype(BF16)


def _fnet_in_call(h, w, cs):
    m, d = h.shape
    n = w.shape[1]
    tm = 512
    spec_o = pl.BlockSpec((tm, n), lambda i: (i, 0))
    return pl.pallas_call(
        _fnet_in_kernel,
        grid=(m // tm,),
        in_specs=[pl.BlockSpec((tm, d), lambda i: (i, 0)),
                  pl.BlockSpec((d, n), lambda i: (0, 0)),
                  pl.BlockSpec(cs.shape, lambda i: (0, 0))],
        out_specs=[spec_o, spec_o],
        out_shape=[jax.ShapeDtypeStruct((m, n), BF16)] * 2,
        compiler_params=_cparams(("arbitrary",)),
        name="inproj_fnet",
    )(h, w, cs)


def _fnet_pos_kernel(xc_c_ref, xs_c_ref, xc_l_ref, xs_l_ref, cc_ref, sc_ref, cl_ref, sl_ref, o_ref,
                     *, n_ctx_tiles):
    i = pl.program_id(0)

    @pl.when(i < n_ctx_tiles)
    def _():
        o_ref[...] = (_dot(cc_ref[...], xc_c_ref[...]) - _dot(sc_ref[...], xs_c_ref[...])).astype(BF16)

    @pl.when(i >= n_ctx_tiles)
    def _():
        o_ref[...] = (_dot(cl_ref[...], xc_l_ref[...]) - _dot(sl_ref[...], xs_l_ref[...])).astype(BF16)


def _fnet_pos_call(xc, xs, cc, sc, cl, sl, ctx_rows):
    m, n = xc.shape
    tm = cc.shape[0]
    lat_len = cl.shape[0]
    n_ctx = ctx_rows // tm
    per_lat = lat_len // tm
    ctx_spec = pl.BlockSpec((tm, n), lambda i: (jnp.minimum(i, n_ctx - 1), 0))
    lat_spec = pl.BlockSpec(
        (lat_len, n), lambda i: (ctx_rows // lat_len + jnp.maximum(i - n_ctx, 0) // per_lat, 0))
    dft_c = pl.BlockSpec((tm, tm), lambda i: (0, 0))
    dft_l = pl.BlockSpec((tm, lat_len), lambda i: (jnp.maximum(i - n_ctx, 0) % per_lat, 0))
    kern = functools.partial(_fnet_pos_kernel, n_ctx_tiles=n_ctx)
    return pl.pallas_call(
        kern,
        grid=(m // tm,),
        in_specs=[ctx_spec, ctx_spec, lat_spec, lat_spec, dft_c, dft_c, dft_l, dft_l],
        out_specs=pl.BlockSpec((tm, n), lambda i: (i, 0)),
        out_shape=jax.ShapeDtypeStruct((m, n), BF16),
        compiler_params=_cparams(("arbitrary",)),
        name="fnet_pos",
    )(xc, xs, xc, xs, cc, sc, cl, sl)


def _sgu_kernel(h_ref, w_ref, g_ref, wsp_ref, bsp_ref, o_ref):
    tm, width = o_ref.shape
    hd = width // SGU_HEADS
    acc = _dot(h_ref[...], w_ref[...])
    k0 = math.sqrt(2.0 / math.pi)
    s = 0.5 * acc * (1.0 + jnp.tanh(k0 * (acc + 0.044715 * (acc * acc * acc))))
    u = s[:, :width]
    v = _rms(s[:, width:], g_ref[...]).astype(BF16)
    for k in range(tm // CHUNK):
        rows = slice(k * CHUNK, (k + 1) * CHUNK)
        for hh in range(SGU_HEADS):
            cols = slice(hh * hd, (hh + 1) * hd)
            sp = _dot(wsp_ref[hh], v[rows, cols]) + bsp_ref[:, hh:hh + 1]
            o_ref[rows, cols] = (u[rows, cols] * sp).astype(BF16)


def _sgu_call(h, w, g_sgu, w_sp, b_sp_t):
    m, d = h.shape
    n = w.shape[1]
    width = n // 2
    tm = 512
    return pl.pallas_call(
        _sgu_kernel,
        grid=(m // tm,),
        in_specs=[pl.BlockSpec((tm, d), lambda i: (i, 0)),
                  pl.BlockSpec((d, n), lambda i: (0, 0)),
                  pl.BlockSpec((1, width), lambda i: (0, 0)),
                  pl.BlockSpec(w_sp.shape, lambda i: (0, 0, 0)),
                  pl.BlockSpec(b_sp_t.shape, lambda i: (0, 0))],
        out_specs=pl.BlockSpec((tm, width), lambda i: (i, 0)),
        out_shape=jax.ShapeDtypeStruct((m, width), BF16),
        compiler_params=_cparams(("arbitrary",)),
        name="inproj_sgu",
    )(h, w, g_sgu.reshape(1, width), w_sp, b_sp_t)


def _ssd_kernel(tbl_ref, xs_ref, b_ref, c_ref, dt_ref, h0_ref, e_ref, alogx_ref, alog_ref, *rest,
                reverse, final, n_ctx_seq, lane_off):
    if final:
        yprev_ref, zs_ref, dsk_ref, gssd_ref, y_ref, hfin_ref, st_ref, ybuf_ref = rest
    else:
        y_ref, hfin_ref, st_ref = rest
    s = pl.program_id(0)
    seq = tbl_ref[1, s]
    first = tbl_ref[2, s]
    last = tbl_ref[3, s]
    T = CHUNK
    hp = xs_ref.shape[1]
    gw = hp // SSD_GROUPS
    n_heads = hp // SSD_HEAD_DIM

    @pl.when(first == 1)
    def _():
        h0 = h0_ref[0, 0, 0].reshape(hp, SSD_STATE)
        h0 = jnp.where(seq < n_ctx_seq, 0.0, h0)
        st_ref[...] = h0.T

    row = lax.broadcasted_iota(jnp.int32, (T, T), 0)
    col = lax.broadcasted_iota(jnp.int32, (T, T), 1)
    tri = (row <= col) if reverse else (row >= col)
    tri_bf = jnp.where(tri, 1.0, 0.0).astype(BF16)

    dt = dt_ref[...]
    dtx = _dot_exact_rhs(dt, e_ref[...])
    ax = dtx * (-jnp.exp(alogx_ref[...]))
    acsx = _dot_exact_lhs(tri_bf, ax)
    acs = _dot_exact_lhs(tri_bf, dt * (-jnp.exp(alog_ref[...])))
    acs_t = acs.T

    xs = xs_ref[...].astype(F32)
    xp = xs * dtx
    xp_bf = xp.astype(BF16)
    end = acsx[0:1, :] if reverse else acsx[T - 1:T, :]
    xpd = (xp * jnp.exp(end - acsx)).astype(BF16)
    ex = jnp.exp(acsx)
    dec = jnp.exp(end)
    st = st_ref[...]
    st_bf = st.astype(BF16)
    lane = lax.broadcasted_iota(jnp.int32, (T, LANES), 1)
    low_half = lane < SSD_HEAD_DIM
    if final:
        dsum = dsk_ref[0:1, :] + dsk_ref[1:2, :]
        ssq = jnp.zeros((T, 1), F32)

    heads_per_group = n_heads // SSD_GROUPS
    for g in range(SSD_GROUPS):
        bg = b_ref[:, g * SSD_STATE:(g + 1) * SSD_STATE]
        cg = c_ref[:, g * SSD_STATE:(g + 1) * SSD_STATE]
        gs = slice(g * gw, (g + 1) * gw)
        cb = lax.dot_general(cg, bg, (((1,), (1,)), ((), ())), preferred_element_type=F32)
        yoff = _dot(cg, st_bf[:, gs])
        stg = lax.dot_general(bg, xpd[:, gs], (((0,), (0,)), ((), ())), preferred_element_type=F32)
        st_ref[:, gs] = st[:, gs] * dec[:, gs] + stg
        for q in range(heads_per_group // 2):
            ms = []
            for r in (2 * q, 2 * q + 1):
                hl = lane_off + g * heads_per_group + r
                diff = acs[:, hl:hl + 1] - acs_t[hl:hl + 1, :]
                lm = jnp.exp(jnp.where(tri, diff, -jnp.inf))
                ms.append((cb * lm).astype(BF16))
            c0 = g * gw + q * LANES
            cs_ = slice(c0, c0 + LANES)
            xpair = xp_bf[:, cs_]
            zero = jnp.zeros_like(xpair)
            rhs = jnp.concatenate([jnp.where(low_half, xpair, zero), jnp.where(low_half, zero, xpair)],
                                  axis=0)
            yd = _dot(jnp.concatenate(ms, axis=1), rhs)
            yblk = yd + yoff[:, q * LANES:(q + 1) * LANES] * ex[:, cs_]
            if final:
                t = (yblk + yprev_ref[:, cs_] + xs[:, cs_] * dsum[:, cs_]) * zs_ref[:, cs_].astype(F32)
                ssq = ssq + jnp.sum(t * t, axis=-1, keepdims=True)
                ybuf_ref[:, cs_] = t
            else:
                y_ref[:, cs_] = yblk

    if final:
        scale = lax.rsqrt(ssq / hp + EPS)
        y_ref[...] = (ybuf_ref[...] * scale * gssd_ref[...]).astype(BF16)

    @pl.when(last == 1)
    def _():
        hfin_ref[0] = st_ref[...].T


def _ssd_tables(n_ctx_seq, ctx_chunks, n_lat_seq, lat_chunks, reverse):
    blk, seq, first, last = [], [], [], []
    base = 0
    for sidx, nch in [(i, ctx_chunks) for i in range(n_ctx_seq)] + \
                     [(n_ctx_seq + i, lat_chunks) for i in range(n_lat_seq)]:
        order = list(range(nch))[::-1] if reverse else list(range(nch))
        for k, cidx in enumerate(order):
            blk.append(base + cidx)
            seq.append(sidx)
            first.append(1 if k == 0 else 0)
            last.append(1 if k == nch - 1 else 0)
        base += nch
    return np.array([blk, seq, first, last], dtype=np.int32)


def _ssd_call(xbc, dt, state_ssd, layer, e_mat, alogx, alog128, geom, *, reverse, final, extra=()):
    n_ctx_seq, ctx_chunks, n_lat_seq, lat_chunks = geom
    m = xbc.shape[0]
    hp = e_mat.shape[1]
    n_groups_cols = SSD_GROUPS * SSD_STATE
    tbl = jnp.asarray(_ssd_tables(n_ctx_seq, ctx_chunks, n_lat_seq, lat_chunks, reverse))
    steps = tbl.shape[1]
    n_seq = n_ctx_seq + n_lat_seq
    direction = 1 if reverse else 0
    n_heads = hp // SSD_HEAD_DIM
    kern = functools.partial(_ssd_kernel, reverse=reverse, final=final, n_ctx_seq=n_ctx_seq,
                             lane_off=direction * n_heads)
    rowblk = lambda width, cblk: pl.BlockSpec((CHUNK, width), lambda s, t: (t[0, s], cblk))
    const2 = lambda shape: pl.BlockSpec(shape, lambda s, t: (0, 0))
    in_specs = [
        rowblk(hp, 0),
        rowblk(n_groups_cols, hp // n_groups_cols),
        rowblk(n_groups_cols, hp // n_groups_cols + 1),
        rowblk(LANES, 0),
        pl.BlockSpec((1, 1, 1) + state_ssd.shape[3:],
                     lambda s, t: (jnp.maximum(t[1, s] - n_ctx_seq, 0), layer, direction, 0, 0, 0)),
        const2(e_mat.shape), const2(alogx.shape), const2(alog128.shape),
    ]
    args = [xbc, xbc, xbc, dt, state_ssd, e_mat, alogx, alog128]
    scratch = [pltpu.VMEM((SSD_STATE, hp), F32)]
    if final:
        yprev, zs, dsk, gssd = extra
        in_specs += [rowblk(hp, 0), rowblk(hp, 0), const2(dsk.shape), const2(gssd.shape)]
        args += [yprev, zs, dsk, gssd]
        scratch.append(pltpu.VMEM((CHUNK, hp), F32))
    y_dtype = BF16 if final else F32
    grid_spec = pltpu.PrefetchScalarGridSpec(
        num_scalar_prefetch=1,
        grid=(steps,),
        in_specs=in_specs,
        out_specs=[rowblk(hp, 0),
                   pl.BlockSpec((1, hp, SSD_STATE), lambda s, t: (t[1, s], 0, 0))],
        scratch_shapes=scratch,
    )
    return pl.pallas_call(
        kern,
        grid_spec=grid_spec,
        out_shape=[jax.ShapeDtypeStruct((m, hp), y_dtype),
                   jax.ShapeDtypeStruct((n_seq, hp, SSD_STATE), F32)],
        compiler_params=_cparams(("arbitrary",)),
        name="ssd_bwd" if reverse else "ssd_fwd",
    )(tbl, *args)


def _mixout_kernel(ys_ref, yf_ref, yg_ref, x_ref, w_ref, mod_ref, ng_ref, xo_ref, ho_ref, acc_ref):
    k = pl.program_id(1)
    kc = w_ref.shape[0]

    @pl.when(k == 0)
    def _():
        acc_ref[...] = _dot(ys_ref[:, :kc], w_ref[...])

    @pl.when(k == 1)
    def _():
        acc_ref[...] += _dot(ys_ref[:, kc:], w_ref[...])

    @pl.when(k == 2)
    def _():
        acc_ref[...] += _dot(yf_ref[...], w_ref[...])

    @pl.when(k == 3)
    def _():
        mm = acc_ref[...] + _dot(yg_ref[...], w_ref[...])
        xn = x_ref[...] + mod_ref[0, 5:6, :] * _rms(mm, ng_ref[3:4, :])
        xo_ref[...] = xn
        hn = _rms(xn, ng_ref[4:5, :]) * (1.0 + mod_ref[0, 7:8, :]) + mod_ref[0, 6:7, :]
        ho_ref[...] = hn.astype(BF16)


def _mixout_call(yssd, yfnet, ysgu, x, w_out, mod, ng, rows_per_group):
    m, d = x.shape
    kc = yfnet.shape[1]
    assert yssd.shape[1] == 2 * kc and ysgu.shape[1] == kc and w_out.shape[0] == 4 * kc
    tm = 512
    row = lambda width: pl.BlockSpec((tm, width), lambda i, k: (i, 0))
    return pl.pallas_call(
        _mixout_kernel,
        grid=(m // tm, 4),
        in_specs=[row(2 * kc), row(kc), row(kc), row(d),
                  pl.BlockSpec((kc, d), lambda i, k: (k, 0)),
                  pl.BlockSpec((1, N_MOD, d), lambda i, k: (i * tm // rows_per_group, 0, 0)),
                  pl.BlockSpec(ng.shape, lambda i, k: (0, 0))],
        out_specs=[row(d), row(d)],
        out_shape=[jax.ShapeDtypeStruct((m, d), F32), jax.ShapeDtypeStruct((m, d), BF16)],
        scratch_shapes=[pltpu.VMEM((tm, d), F32)],
        compiler_params=_cparams(("arbitrary", "arbitrary")),
        name="mixout",
    )(yssd, yfnet, ysgu, x, w_out, mod, ng)


def _dft_tables(n):
    j = lax.broadcasted_iota(jnp.int32, (n, n), 0)
    k = lax.broadcasted_iota(jnp.int32, (n, n), 1)
    ang = ((j * k) % n).astype(F32) * (2.0 * math.pi / n)
    scale = 1.0 / math.sqrt(n)
    return (jnp.cos(ang) * scale).astype(BF16), (jnp.sin(ang) * scale).astype(BF16)


def kernel(x_prompt, x_sample, state_ssd, c, c_ctx, w_mod, b_mod, norm_g, w_ffn1_up, w_ffn1_down,
           w_in, conv_w, conv_b, a_log, dt_bias, d_skip, g_ssd, g_sgu, w_sp, b_sp, w_out,
           w_ffn2_up, w_ffn2_down):
    batch, seq, d = x_prompt.shape
    dec_batch, dec_seq, _ = x_sample.shape
    depth = w_mod.shape[0]
    ctx_rows = batch * seq
    assert ctx_rows == dec_seq, "row tiles map to modulation groups in units of dec_seq rows"
    n_heads = a_log.shape[2]
    hp = n_heads * SSD_HEAD_DIM
    gn = SSD_GROUPS * SSD_STATE
    off_dt = hp + hp + 2 * gn
    off_fnet = off_dt + 2 * n_heads
    fnet_w = w_out.shape[1] // 4
    off_sgu = off_fnet + fnet_w

    x = jnp.concatenate([x_prompt.reshape(ctx_rows, d), x_sample.reshape(dec_batch * dec_seq, d)], axis=0)
    cvec = jnp.concatenate([c_ctx[None, :], c, jnp.zeros((8 - 1 - dec_batch, d), F32)], axis=0)
    mod_all = _mod_call(cvec, w_mod, b_mod).reshape(depth, 8, N_MOD, d)

    gw = fnet_w // FNET_GROUPS
    cc_g, sc_g = _dft_tables(gw)
    cs_chan = jnp.concatenate([cc_g, sc_g], axis=1)
    cc, sc = _dft_tables(seq)
    cl, sl = _dft_tables(dec_seq)
    e_np = np.zeros((2, LANES, hp), np.float32)
    for dr in range(2):
        for hh in range(n_heads):
            e_np[dr, dr * n_heads + hh, hh * SSD_HEAD_DIM:(hh + 1) * SSD_HEAD_DIM] = 1.0
    e_mats = [jnp.asarray(e_np[0], BF16), jnp.asarray(e_np[1], BF16)]
    geom = (batch, seq // CHUNK, dec_batch, dec_seq // CHUNK)

    h = _prenorm_call(x, mod_all[0], norm_g[0], dec_seq)
    states = []
    for l in range(depth):
        mod = mod_all[l]
        ng = norm_g[l]
        w_in_l = w_in[l]
        w_z = w_in_l[:, :hp].astype(BF16)
        w_xbc = w_in_l[:, hp:off_dt].astype(BF16)
        w_dt = jnp.pad(w_in_l[:, off_dt:off_fnet], ((0, 0), (0, LANES - 2 * n_heads))).astype(BF16)
        w_f = w_in_l[:, off_fnet:off_sgu].astype(BF16)
        w_s = w_in_l[:, off_sgu:].astype(BF16)
        dtb = jnp.pad(dt_bias[l].reshape(1, 2 * n_heads), ((0, 0), (0, LANES - 2 * n_heads)))
        alog128 = jnp.pad(a_log[l].reshape(1, 2 * n_heads), ((0, 0), (0, LANES - 2 * n_heads)))
        alogx = [jnp.repeat(a_log[l, dr], SSD_HEAD_DIM).reshape(1, hp) for dr in range(2)]
        dskx = jnp.repeat(d_skip[l], SSD_HEAD_DIM, axis=1)

        x, h = _ffn_call(h, x, w_ffn1_up[l].astype(BF16), w_ffn1_down[l].astype(BF16), mod, ng, mod, ng,
                         dec_seq, gate=2, ng_out=1, ng_next=2, sh_next=3, sc_next=4, emit_next=True)

        zs = _z_call(h, w_z)
        xbc = _xbc_call(h, w_xbc, conv_w[l], conv_b[l], ctx_rows, seq)
        dt = _dt_call(h, w_dt, dtb)
        xc, xsn = _fnet_in_call(h, w_f, cs_chan)
        ysgu = _sgu_call(h, w_s, g_sgu[l], w_sp[l].astype(BF16), b_sp[l].T)

        y_f, hfin_f = _ssd_call(xbc, dt, state_ssd, l, e_mats[0], alogx[0], alog128, geom,
                                reverse=False, final=False)
        yssd, hfin_b = _ssd_call(xbc, dt, state_ssd, l, e_mats[1], alogx[1], alog128, geom,
                                 reverse=True, final=True,
                                 extra=(y_f, zs, dskx, g_ssd[l].reshape(1, hp)))
        states.append(jnp.stack([hfin_f[:batch], hfin_b[:batch]], axis=1))

        yfnet = _fnet_pos_call(xc, xsn, cc, sc, cl, sl, ctx_rows)
        x, h = _mixout_call(yssd, yfnet, ysgu, x, w_out[l].astype(BF16), mod, ng, dec_seq)

        last = l == depth - 1
        nxt = l if last else l + 1
        res = _ffn_call(h, x, w_ffn2_up[l].astype(BF16), w_ffn2_down[l].astype(BF16), mod, ng,
                        mod_all[nxt], norm_g[nxt], dec_seq, gate=8, ng_out=5, ng_next=0,
                        sh_next=0, sc_next=1, emit_next=not last)
        if last:
            x = res[0] if isinstance(res, (list, tuple)) else res
        else:
            x, h = res

    y_prompt = x[:ctx_rows].reshape(batch, seq, d)
    y_sample = x[ctx_rows:].reshape(dec_batch, dec_seq, d)
    new_state = jnp.stack(states, axis=1).reshape(
        batch, depth, 2, n_heads, SSD_HEAD_DIM, SSD_STATE).astype(x_prompt.dtype)
    return (y_prompt, y_sample, new_state)
```

```python
import functools
import math

import numpy as np
import jax
import jax.numpy as jnp
from jax import lax
from jax.experimental import pallas as pl
from jax.experimental.pallas import tpu as pltpu

F32 = jnp.float32
BF16 = jnp.bfloat16
EPS = 1e-6

LANES = 128
VMEM_LIMIT = 56 * 1024 * 1024

SSD_HEAD_DIM = 64
SSD_GROUPS = 4
SSD_STATE = 128
CHUNK = 128
CONV_K = 5
GRID_W = 64
FNET_GROUPS = 4
SGU_HEADS = 8
N_MOD = 9


def _cparams(sem):
    return pltpu.CompilerParams(dimension_semantics=sem, vmem_limit_bytes=VMEM_LIMIT)


def _rms(x, g):
    ms = jnp.mean(x * x, axis=-1, keepdims=True)
    return x * lax.rsqrt(ms + EPS) * g


def _silu(x):
    return x * (1.0 / (1.0 + jnp.exp(-x)))


def _split3(v):
    hi = v.astype(BF16)
    r1 = v - hi.astype(F32)
    mid = r1.astype(BF16)
    lo = (r1 - mid.astype(F32)).astype(BF16)
    return hi, mid, lo


def _dot(a, b):
    return jnp.dot(a, b, preferred_element_type=F32)


def _dot_exact_rhs(v, m_bf):
    hi, mid, lo = _split3(v)
    return _dot(hi, m_bf) + _dot(mid, m_bf) + _dot(lo, m_bf)


def _dot_exact_lhs(m_bf, v):
    hi, mid, lo = _split3(v)
    return _dot(m_bf, hi) + _dot(m_bf, mid) + _dot(m_bf, lo)


def _mod_kernel(c_ref, w_ref, b_ref, o_ref):
    s = _silu(c_ref[...]).astype(BF16)
    o_ref[0] = _dot(s, w_ref[0].astype(BF16)) + b_ref[0]


def _mod_call(cvec, w_mod, b_mod):
    depth, d, n = w_mod.shape
    tn = 1024
    return pl.pallas_call(
        _mod_kernel,
        grid=(depth, n // tn),
        in_specs=[
            pl.BlockSpec((8, d), lambda l, j: (0, 0)),
            pl.BlockSpec((1, d, tn), lambda l, j: (l, 0, j)),
            pl.BlockSpec((1, 1, tn), lambda l, j: (l, 0, j)),
        ],
        out_specs=pl.BlockSpec((1, 8, tn), lambda l, j: (l, 0, j)),
        out_shape=jax.ShapeDtypeStruct((depth, 8, n), F32),
        compiler_params=_cparams(("arbitrary", "arbitrary")),
        name="mod",
    )(cvec, w_mod, b_mod.reshape(depth, 1, n))


def _prenorm_kernel(x_ref, mod_ref, ng_ref, h_ref):
    h = _rms(x_ref[...], ng_ref[0:1, :]) * (1.0 + mod_ref[0, 1:2, :]) + mod_ref[0, 0:1, :]
    h_ref[...] = h.astype(BF16)


def _prenorm_call(x, mod, ng, rows_per_group):
    m, d = x.shape
    tm = 512
    return pl.pallas_call(
        _prenorm_kernel,
        grid=(m // tm,),
        in_specs=[
            pl.BlockSpec((tm, d), lambda i: (i, 0)),
            pl.BlockSpec((1, N_MOD, d), lambda i: (i * tm // rows_per_group, 0, 0)),
            pl.BlockSpec(ng.shape, lambda i: (0, 0)),
        ],
        out_specs=pl.BlockSpec((tm, d), lambda i: (i, 0)),
        out_shape=jax.ShapeDtypeStruct((m, d), BF16),
        compiler_params=_cparams(("arbitrary",)),
        name="prenorm",
    )(x, mod, ng)


def _ffn_kernel(h_ref, x_ref, wg_ref, wu_ref, wd_ref, mod_ref, ng_ref, modn_ref, ngn_ref,
                *rest, gate, ng_out, ng_next, sh_next, sc_next, emit_next):
    if emit_next:
        xo_ref, ho_ref, acc_ref, act_ref = rest
    else:
        xo_ref, acc_ref, act_ref = rest
    j = pl.program_id(1)
    nf = pl.num_programs(1) - 1
    d_model = acc_ref.shape[1]
    n_split = 4
    cw = d_model // n_split

    @pl.when(j == 0)
    def _():
        act_ref[1] = jnp.zeros(act_ref.shape[1:], BF16)
        acc_ref[...] = jnp.zeros(acc_ref.shape, F32)

    @pl.when(j < nf)
    def _():
        h = h_ref[...]
        g = _dot(h, wg_ref[...])
        u = _dot(h, wu_ref[...])
        a_prev = act_ref[(j + 1) % 2]
        for cblk in range(n_split):
            cols = slice(cblk * cw, (cblk + 1) * cw)
            acc_ref[:, cols] += _dot(a_prev, wd_ref[:, cols])
        act_ref[j % 2] = (_silu(g) * u).astype(BF16)

    @pl.when(j == nf)
    def _():
        y = acc_ref[...] + _dot(act_ref[(j + 1) % 2], wd_ref[...])
        xn = x_ref[...] + 0.5 * mod_ref[0, gate:gate + 1, :] * _rms(y, ng_ref[ng_out:ng_out + 1, :])
        xo_ref[...] = xn
        if emit_next:
            hn = (_rms(xn, ngn_ref[ng_next:ng_next + 1, :])
                  * (1.0 + modn_ref[0, sc_next:sc_next + 1, :]) + modn_ref[0, sh_next:sh_next + 1, :])
            ho_ref[...] = hn.astype(BF16)


def _ffn_call(h, x, w_up, w_down, mod, ng, mod_next, ng_next_arr, rows_per_group, *, gate, ng_out,
              ng_next, sh_next, sc_next, emit_next):
    m, d = x.shape
    f = w_down.shape[0]
    tm, fc = 512, 512
    nf = f // fc
    kern = functools.partial(_ffn_kernel, gate=gate, ng_out=ng_out, ng_next=ng_next,
                             sh_next=sh_next, sc_next=sc_next, emit_next=emit_next)
    row = pl.BlockSpec((tm, d), lambda i, j: (i, 0))
    modspec = pl.BlockSpec((1, N_MOD, d), lambda i, j: (i * tm // rows_per_group, 0, 0))
    ngspec = pl.BlockSpec(ng.shape, lambda i, j: (0, 0))
    out_specs = [row]
    out_shape = [jax.ShapeDtypeStruct((m, d), F32)]
    if emit_next:
        out_specs.append(row)
        out_shape.append(jax.ShapeDtypeStruct((m, d), BF16))
    return pl.pallas_call(
        kern,
        grid=(m // tm, nf + 1),
        in_specs=[
            row, row,
            pl.BlockSpec((d, fc), lambda i, j: (0, jnp.minimum(j, nf - 1))),
            pl.BlockSpec((d, fc), lambda i, j: (0, jnp.minimum(j, nf - 1) + nf)),
            pl.BlockSpec((fc, d), lambda i, j: (jnp.maximum(j - 1, 0), 0)),
            modspec, ngspec, modspec, ngspec,
        ],
        out_specs=out_specs,
        out_shape=out_shape,
        scratch_shapes=[pltpu.VMEM((tm, d), F32), pltpu.VMEM((2, tm, fc), BF16)],
        compiler_params=_cparams(("arbitrary", "arbitrary")),
        name="ffn",
    )(h, x, w_up, w_up, w_down, mod, ng, mod_next, ng_next_arr)


def _z_kernel(h_ref, w_ref, o_ref):
    o_ref[...] = _silu(_dot(h_ref[...], w_ref[...])).astype(BF16)


def _z_call(h, w):
    m, d = h.shape
    n = w.shape[1]
    tm, tn = 1024, 1024
    return pl.pallas_call(
        _z_kernel,
        grid=(m // tm, n // tn),
        in_specs=[pl.BlockSpec((tm, d), lambda i, j: (i, 0)),
                  pl.BlockSpec((d, tn), lambda i, j: (0, j))],
        out_specs=pl.BlockSpec((tm, tn), lambda i, j: (i, j)),
        out_shape=jax.ShapeDtypeStruct((m, n), BF16),
        compiler_params=_cparams(("arbitrary", "arbitrary")),
        name="inproj_z",
    )(h, w)


def _xbc_kernel(h_ref, w_ref, cw_ref, cb_ref, o_ref, *, ctx_rows, ctx_seg, lat_seg):
    tm, tn = o_ref.shape
    i = pl.program_id(0)
    acc = _dot(h_ref[...], w_ref[...])
    seg = jnp.where(i * tm < ctx_rows, ctx_seg, lat_seg)
    pos = lax.broadcasted_iota(jnp.int32, (tm, LANES), 0) & (seg - 1)
    half = CONV_K // 2
    valid = {dlt: (pos + dlt >= 0) & (pos + dlt < seg) for dlt in range(-half, half + 1) if dlt != 0}
    for cblk in range(tn // LANES):
        sl = slice(cblk * LANES, (cblk + 1) * LANES)
        u = acc[:, sl]
        out = u * cw_ref[half:half + 1, sl] + cb_ref[0:1, sl]
        for dlt in range(-half, half + 1):
            if dlt == 0:
                continue
            shifted = pltpu.roll(u, (-dlt) % tm, 0)
            out = out + jnp.where(valid[dlt], shifted, 0.0) * cw_ref[half + dlt:half + dlt + 1, sl]
        o_ref[:, sl] = _silu(out).astype(BF16)


def _xbc_call(h, w, conv_w, conv_b, ctx_rows, ctx_seg):
    m, d = h.shape
    n = w.shape[1]
    tm, tn = 512, 1024
    kern = functools.partial(_xbc_kernel, ctx_rows=ctx_rows, ctx_seg=ctx_seg, lat_seg=GRID_W)
    return pl.pallas_call(
        kern,
        grid=(m // tm, n // tn),
        in_specs=[pl.BlockSpec((tm, d), lambda i, j: (i, 0)),
                  pl.BlockSpec((d, tn), lambda i, j: (0, j)),
                  pl.BlockSpec((CONV_K, tn), lambda i, j: (0, j)),
                  pl.BlockSpec((1, tn), lambda i, j: (0, j))],
        out_specs=pl.BlockSpec((tm, tn), lambda i, j: (i, j)),
        out_shape=jax.ShapeDtypeStruct((m, n), BF16),
        compiler_params=_cparams(("arbitrary", "arbitrary")),
        name="inproj_xbc",
    )(h, w, conv_w, conv_b.reshape(1, n))


def _dt_kernel(h_ref, w_ref, b_ref, o_ref):
    v = _dot(h_ref[...], w_ref[...]) + b_ref[...]
    o_ref[...] = jnp.maximum(v, 0.0) + jnp.log1p(jnp.exp(-jnp.abs(v)))


def _dt_call(h, w, b):
    m, d = h.shape
    n = w.shape[1]
    tm = 1024
    return pl.pallas_call(
        _dt_kernel,
        grid=(m // tm,),
        in_specs=[pl.BlockSpec((tm, d), lambda i: (i, 0)),
                  pl.BlockSpec((d, n), lambda i: (0, 0)),
                  pl.BlockSpec((1, n), lambda i: (0, 0))],
        out_specs=pl.BlockSpec((tm, n), lambda i: (i, 0)),
        out_shape=jax.ShapeDtypeStruct((m, n), F32),
        compiler_params=_cparams(("arbitrary",)),
        name="inproj_dt",
    )(h, w, b)


def _fnet_in_kernel(h_ref, w_ref, cs_ref, xc_ref, xs_ref):
    f = _dot(h_ref[...], w_ref[...]).astype(BF16)
    gw = cs_ref.shape[0]
    for g in range(FNET_GROUPS):
        sl = slice(g * gw, (g + 1) * gw)
        r = _dot(f[:, sl], cs_ref[...])
        xc_ref[:, sl] = r[:, :gw].astype(BF16)
        xs_ref[:, sl] = r[:, gw:].astype(BF16)


def _fnet_in_call(h, w, cs):
    m, d = h.shape
    n = w.shape[1]
    tm = 512
    spec_o = pl.BlockSpec((tm, n), lambda i: (i, 0))
    return pl.pallas_call(
        _fnet_in_kernel,
        grid=(m // tm,),
        in_specs=[pl.BlockSpec((tm, d), lambda i: (i, 0)),
                  pl.BlockSpec((d, n), lambda i: (0, 0)),
                  pl.BlockSpec(cs.shape, lambda i: (0, 0))],
        out_specs=[spec_o, spec_o],
        out_shape=[jax.ShapeDtypeStruct((m, n), BF16)] * 2,
        compiler_params=_cparams(("arbitrary",)),
        name="inproj_fnet",
    )(h, w, cs)


def _fnet_pos_kernel(xc_c_ref, xs_c_ref, xc_l_ref, xs_l_ref, cc_ref, sc_ref, cl_ref, sl_ref, o_ref,
                     *, n_ctx_tiles):
    i = pl.program_id(0)

    @pl.when(i < n_ctx_tiles)
    def _():
        o_ref[...] = (_dot(cc_ref[...], xc_c_ref[...]) - _dot(sc_ref[...], xs_c_ref[...])).astype(BF16)

    @pl.when(i >= n_ctx_tiles)
    def _():
        o_ref[...] = (_dot(cl_ref[...], xc_l_ref[...]) - _dot(sl_ref[...], xs_l_ref[...])).astype(BF16)


def _fnet_pos_call(xc, xs, cc, sc, cl, sl, ctx_rows):
    m, n = xc.shape
    tm = cc.shape[0]
    lat_len = cl.shape[0]
    n_ctx = ctx_rows // tm
    per_lat = lat_len // tm
    ctx_spec = pl.BlockSpec((tm, n), lambda i: (jnp.minimum(i, n_ctx - 1), 0))
    lat_spec = pl.BlockSpec(
        (lat_len, n), lambda i: (ctx_rows // lat_len + jnp.maximum(i - n_ctx, 0) // per_lat, 0))
    dft_c = pl.BlockSpec((tm, tm), lambda i: (0, 0))
    dft_l = pl.BlockSpec((tm, lat_len), lambda i: (jnp.maximum(i - n_ctx, 0) % per_lat, 0))
    kern = functools.partial(_fnet_pos_kernel, n_ctx_tiles=n_ctx)
    return pl.pallas_call(
        kern,
        grid=(m // tm,),
        in_specs=[ctx_spec, ctx_spec, lat_spec, lat_spec, dft_c, dft_c, dft_l, dft_l],
        out_specs=pl.BlockSpec((tm, n), lambda i: (i, 0)),
        out_shape=jax.ShapeDtypeStruct((m, n), BF16),
        compiler_params=_cparams(("arbitrary",)),
        name="fnet_pos",
    )(xc, xs, xc, xs, cc, sc, cl, sl)


def _sgu_kernel(h_ref, w_ref, g_ref, wsp_ref, bsp_ref, o_ref):
    tm, width = o_ref.shape
    hd = width // SGU_HEADS
    acc = _dot(h_ref[...], w_ref[...])
    k0 = math.sqrt(2.0 / math.pi)
    s = 0.5 * acc * (1.0 + jnp.tanh(k0 * (acc + 0.044715 * (acc * acc * acc))))
    u = s[:, :width]
    v = _rms(s[:, width:], g_ref[...]).astype(BF16)
    for k in range(tm // CHUNK):
        rows = slice(k * CHUNK, (k + 1) * CHUNK)
        for hh in range(SGU_HEADS):
            cols = slice(hh * hd, (hh + 1) * hd)
            sp = _dot(wsp_ref[hh], v[rows, cols]) + bsp_ref[:, hh:hh + 1]
            o_ref[rows, cols] = (u[rows, cols] * sp).astype(BF16)


def _sgu_call(h, w, g_sgu, w_sp, b_sp_t):
    m, d = h.shape
    n = w.shape[1]
    width = n // 2
    tm = 512
    return pl.pallas_call(
        _sgu_kernel,
        grid=(m // tm,),
        in_specs=[pl.BlockSpec((tm, d), lambda i: (i, 0)),
                  pl.BlockSpec((d, n), lambda i: (0, 0)),
                  pl.BlockSpec((1, width), lambda i: (0, 0)),
                  pl.BlockSpec(w_sp.shape, lambda i: (0, 0, 0)),
                  pl.BlockSpec(b_sp_t.shape, lambda i: (0, 0))],
        out_specs=pl.BlockSpec((tm, width), lambda i: (i, 0)),
        out_shape=jax.ShapeDtypeStruct((m, width), BF16),
        compiler_params=_cparams(("arbitrary",)),
        name="inproj_sgu",
    )(h, w, g_sgu.reshape(1, width), w_sp, b_sp_t)


def _ssd_kernel(tbl_ref, xs_ref, b_ref, c_ref, dt_ref, h0_ref, e_ref, alogx_ref, alog_ref, *rest,
                reverse, final, n_ctx_seq, lane_off):
    if final:
        yprev_ref, zs_ref, dsk_ref, gssd_ref, y_ref, hfin_ref, st_ref, ybuf_ref = rest
    else:
        y_ref, hfin_ref, st_ref = rest
    s = pl.program_id(0)
    seq = tbl_ref[1, s]
    first = tbl_ref[2, s]
    last = tbl_ref[3, s]
    T = CHUNK
    hp = xs_ref.shape[1]
    gw = hp // SSD_GROUPS
    n_heads = hp // SSD_HEAD_DIM

    @pl.when(first == 1)
    def _():
        h0 = h0_ref[0, 0, 0].reshape(hp, SSD_STATE)
        h0 = jnp.where(seq < n_ctx_seq, 0.0, h0)
        st_ref[...] = h0.T

    row = lax.broadcasted_iota(jnp.int32, (T, T), 0)
    col = lax.broadcasted_iota(jnp.int32, (T, T), 1)
    tri = (row <= col) if reverse else (row >= col)
    tri_bf = jnp.where(tri, 1.0, 0.0).astype(BF16)

    dt = dt_ref[...]
    dtx = _dot_exact_rhs(dt, e_ref[...])
    ax = dtx * (-jnp.exp(alogx_ref[...]))
    acsx = _dot_exact_lhs(tri_bf, ax)
    acs = _dot_exact_lhs(tri_bf, dt * (-jnp.exp(alog_ref[...])))
    acs_t = acs.T

    xs = xs_ref[...].astype(F32)
    xp = xs * dtx
    xp_bf = xp.astype(BF16)
    end = acsx[0:1, :] if reverse else acsx[T - 1:T, :]
    xpd = (xp * jnp.exp(end - acsx)).astype(BF16)
    ex = jnp.exp(acsx)
    dec = jnp.exp(end)
    st = st_ref[...]
    st_bf = st.astype(BF16)
    lane = lax.broadcasted_iota(jnp.int32, (T, LANES), 1)
    low_half = lane < SSD_HEAD_DIM
    if final:
        dsum = dsk_ref[0:1, :] + dsk_ref[1:2, :]
        ssq = jnp.zeros((T, 1), F32)

    heads_per_group = n_heads // SSD_GROUPS
    for g in range(SSD_GROUPS):
        bg = b_ref[:, g * SSD_STATE:(g + 1) * SSD_STATE]
        cg = c_ref[:, g * SSD_STATE:(g + 1) * SSD_STATE]
        gs = slice(g * gw, (g + 1) * gw)
        cb = lax.dot_general(cg, bg, (((1,), (1,)), ((), ())), preferred_element_type=F32)
        yoff = _dot(cg, st_bf[:, gs])
        stg = lax.dot_general(bg, xpd[:, gs], (((0,), (0,)), ((), ())), preferred_element_type=F32)
        st_ref[:, gs] = st[:, gs] * dec[:, gs] + stg
        for q in range(heads_per_group // 2):
            ms = []
            for r in (2 * q, 2 * q + 1):
                hl = lane_off + g * heads_per_group + r
                diff = acs[:, hl:hl + 1] - acs_t[hl:hl + 1, :]
                lm = jnp.exp(jnp.where(tri, diff, -jnp.inf))
                ms.append((cb * lm).astype(BF16))
            c0 = g * gw + q * LANES
            cs_ = slice(c0, c0 + LANES)
            xpair = xp_bf[:, cs_]
            zero = jnp.zeros_like(xpair)
            rhs = jnp.concatenate([jnp.where(low_half, xpair, zero), jnp.where(low_half, zero, xpair)],
                                  axis=0)
            yd = _dot(jnp.concatenate(ms, axis=1), rhs)
            yblk = yd + yoff[:, q * LANES:(q + 1) * LANES] * ex[:, cs_]
            if final:
                t = (yblk + yprev_ref[:, cs_] + xs[:, cs_] * dsum[:, cs_]) * zs_ref[:, cs_].astype(F32)
                ssq = ssq + jnp.sum(t * t, axis=-1, keepdims=True)
                ybuf_ref[:, cs_] = t
            else:
                y_ref[:, cs_] = yblk

    if final:
        scale = lax.rsqrt(ssq / hp + EPS)
        y_ref[...] = (ybuf_ref[...] * scale * gssd_ref[...]).astype(BF16)

    @pl.when(last == 1)
    def _():
        hfin_ref[0] = st_ref[...].T


def _ssd_tables(n_ctx_seq, ctx_chunks, n_lat_seq, lat_chunks, reverse):
    blk, seq, first, last = [], [], [], []
    base = 0
    for sidx, nch in [(i, ctx_chunks) for i in range(n_ctx_seq)] + \
                     [(n_ctx_seq + i, lat_chunks) for i in range(n_lat_seq)]:
        order = list(range(nch))[::-1] if reverse else list(range(nch))
        for k, cidx in enumerate(order):
            blk.append(base + cidx)
            seq.append(sidx)
            first.append(1 if k == 0 else 0)
            last.append(1 if k == nch - 1 else 0)
        base += nch
    return np.array([blk, seq, first, last], dtype=np.int32)


def _ssd_call(xbc, dt, state_ssd, layer, e_mat, alogx, alog128, geom, *, reverse, final, extra=()):
    n_ctx_seq, ctx_chunks, n_lat_seq, lat_chunks = geom
    m = xbc.shape[0]
    hp = e_mat.shape[1]
    n_groups_cols = SSD_GROUPS * SSD_STATE
    tbl = jnp.asarray(_ssd_tables(n_ctx_seq, ctx_chunks, n_lat_seq, lat_chunks, reverse))
    steps = tbl.shape[1]
    n_seq = n_ctx_seq + n_lat_seq
    direction = 1 if reverse else 0
    n_heads = hp // SSD_HEAD_DIM
    kern = functools.partial(_ssd_kernel, reverse=reverse, final=final, n_ctx_seq=n_ctx_seq,
                             lane_off=direction * n_heads)
    rowblk = lambda width, cblk: pl.BlockSpec((CHUNK, width), lambda s, t: (t[0, s], cblk))
    const2 = lambda shape: pl.BlockSpec(shape, lambda s, t: (0, 0))
    in_specs = [
        rowblk(hp, 0),
        rowblk(n_groups_cols, hp // n_groups_cols),
        rowblk(n_groups_cols, hp // n_groups_cols + 1),
        rowblk(LANES, 0),
        pl.BlockSpec((1, 1, 1) + state_ssd.shape[3:],
                     lambda s, t: (jnp.maximum(t[1, s] - n_ctx_seq, 0), layer, direction, 0, 0, 0)),
        const2(e_mat.shape), const2(alogx.shape), const2(alog128.shape),
    ]
    args = [xbc, xbc, xbc, dt, state_ssd, e_mat, alogx, alog128]
    scratch = [pltpu.VMEM((SSD_STATE, hp), F32)]
    if final:
        yprev, zs, dsk, gssd = extra
        in_specs += [rowblk(hp, 0), rowblk(hp, 0), const2(dsk.shape), const2(gssd.shape)]
        args += [yprev, zs, dsk, gssd]
        scratch.append(pltpu.VMEM((CHUNK, hp), F32))
    y_dtype = BF16 if final else F32
    grid_spec = pltpu.PrefetchScalarGridSpec(
        num_scalar_prefetch=1,
        grid=(steps,),
        in_specs=in_specs,
        out_specs=[rowblk(hp, 0),
                   pl.BlockSpec((1, hp, SSD_STATE), lambda s, t: (t[1, s], 0, 0))],
        scratch_shapes=scratch,
    )
    return pl.pallas_call(
        kern,
        grid_spec=grid_spec,
        out_shape=[jax.ShapeDtypeStruct((m, hp), y_dtype),
                   jax.ShapeDtypeStruct((n_seq, hp, SSD_STATE), F32)],
        compiler_params=_cparams(("arbitrary",)),
        name="ssd_bwd" if reverse else "ssd_fwd",
    )(tbl, *args)


def _mixout_kernel(ys_ref, yf_ref, yg_ref, x_ref, w_ref, mod_ref, ng_ref, xo_ref, ho_ref, acc_ref):
    k = pl.program_id(1)
    kc = w_ref.shape[0]

    @pl.when(k == 0)
    def _():
        acc_ref[...] = _dot(ys_ref[:, :kc], w_ref[...])

    @pl.when(k == 1)
    def _():
        acc_ref[...] += _dot(ys_ref[:, kc:], w_ref[...])

    @pl.when(k == 2)
    def _():
        acc_ref[...] += _dot(yf_ref[...], w_ref[...])

    @pl.when(k == 3)
    def _():
        mm = acc_ref[...] + _dot(yg_ref[...], w_ref[...])
        xn = x_ref[...] + mod_ref[0, 5:6, :] * _rms(mm, ng_ref[3:4, :])
        xo_ref[...] = xn
        hn = _rms(xn, ng_ref[4:5, :]) * (1.0 + mod_ref[0, 7:8, :]) + mod_ref[0, 6:7, :]
        ho_ref[...] = hn.astype(BF16)


def _mixout_call(yssd, yfnet, ysgu, x, w_out, mod, ng, rows_per_group):
    m, d = x.shape
    kc = yfnet.shape[1]
    assert yssd.shape[1] == 2 * kc and ysgu.shape[1] == kc and w_out.shape[0] == 4 * kc
    tm = 512
    row = lambda width: pl.BlockSpec((tm, width), lambda i, k: (i, 0))
    return pl.pallas_call(
        _mixout_kernel,
        grid=(m // tm, 4),
        in_specs=[row(2 * kc), row(kc), row(kc), row(d),
                  pl.BlockSpec((kc, d), lambda i, k: (k, 0)),
                  pl.BlockSpec((1, N_MOD, d), lambda i, k: (i * tm // rows_per_group, 0, 0)),
                  pl.BlockSpec(ng.shape, lambda i, k: (0, 0))],
        out_specs=[row(d), row(d)],
        out_shape=[jax.ShapeDtypeStruct((m, d), F32), jax.ShapeDtypeStruct((m, d), BF16)],
        scratch_shapes=[pltpu.VMEM((tm, d), F32)],
        compiler_params=_cparams(("arbitrary", "arbitrary")),
        name="mixout",
    )(yssd, yfnet, ysgu, x, w_out, mod, ng)


def _dft_tables(n):
    j = lax.broadcasted_iota(jnp.int32, (n, n), 0)
    k = lax.broadcasted_iota(jnp.int32, (n, n), 1)
    ang = ((j * k) % n).astype(F32) * (2.0 * math.pi / n)
    scale = 1.0 / math.sqrt(n)
    return (jnp.cos(ang) * scale).astype(BF16), (jnp.sin(ang) * scale).astype(BF16)


def kernel(x_prompt, x_sample, state_ssd, c, c_ctx, w_mod, b_mod, norm_g, w_ffn1_up, w_ffn1_down,
           w_in, conv_w, conv_b, a_log, dt_bias, d_skip, g_ssd, g_sgu, w_sp, b_sp, w_out,
           w_ffn2_up, w_ffn2_down):
    batch, seq, d = x_prompt.shape
    dec_batch, dec_seq, _ = x_sample.shape
    depth = w_mod.shape[0]
    ctx_rows = batch * seq
    assert ctx_rows == dec_seq, "row tiles map to modulation groups in units of dec_seq rows"
    n_heads = a_log.shape[2]
    hp = n_heads * SSD_HEAD_DIM
    gn = SSD_GROUPS * SSD_STATE
    off_dt = hp + hp + 2 * gn
    off_fnet = off_dt + 2 * n_heads
    fnet_w = w_out.shape[1] // 4
    off_sgu = off_fnet + fnet_w

    x = jnp.concatenate([x_prompt.reshape(ctx_rows, d), x_sample.reshape(dec_batch * dec_seq, d)], axis=0)
    cvec = jnp.concatenate([c_ctx[None, :], c, jnp.zeros((8 - 1 - dec_batch, d), F32)], axis=0)
    mod_all = _mod_call(cvec, w_mod, b_mod).reshape(depth, 8, N_MOD, d)

    gw = fnet_w // FNET_GROUPS
    cc_g, sc_g = _dft_tables(gw)
    cs_chan = jnp.concatenate([cc_g, sc_g], axis=1)
    cc, sc = _dft_tables(seq)
    cl, sl = _dft_tables(dec_seq)
    e_np = np.zeros((2, LANES, hp), np.float32)
    for dr in range(2):
        for hh in range(n_heads):
            e_np[dr, dr * n_heads + hh, hh * SSD_HEAD_DIM:(hh + 1) * SSD_HEAD_DIM] = 1.0
    e_mats = [jnp.asarray(e_np[0], BF16), jnp.asarray(e_np[1], BF16)]
    geom = (batch, seq // CHUNK, dec_batch, dec_seq // CHUNK)

    h = _prenorm_call(x, mod_all[0], norm_g[0], dec_seq)
    states = []
    for l in range(depth):
        mod = mod_all[l]
        ng = norm_g[l]
        w_in_l = w_in[l]
        w_z = w_in_l[:, :hp].astype(BF16)
        w_xbc = w_in_l[:, hp:off_dt].astype(BF16)
        w_dt = jnp.pad(w_in_l[:, off_dt:off_fnet], ((0, 0), (0, LANES - 2 * n_heads))).astype(BF16)
        w_f = w_in_l[:, off_fnet:off_sgu].astype(BF16)
        w_s = w_in_l[:, off_sgu:].astype(BF16)
        dtb = jnp.pad(dt_bias[l].reshape(1, 2 * n_heads), ((0, 0), (0, LANES - 2 * n_heads)))
        alog128 = jnp.pad(a_log[l].reshape(1, 2 * n_heads), ((0, 0), (0, LANES - 2 * n_heads)))
        alogx = [jnp.repeat(a_log[l, dr], SSD_HEAD_DIM).reshape(1, hp) for dr in range(2)]
        dskx = jnp.repeat(d_skip[l], SSD_HEAD_DIM, axis=1)

        x, h = _ffn_call(h, x, w_ffn1_up[l].astype(BF16), w_ffn1_down[l].astype(BF16), mod, ng, mod, ng,
                         dec_seq, gate=2, ng_out=1, ng_next=2, sh_next=3, sc_next=4, emit_next=True)

        zs = _z_call(h, w_z)
        xbc = _xbc_call(h, w_xbc, conv_w[l], conv_b[l], ctx_rows, seq)
        dt = _dt_call(h, w_dt, dtb)
        xc, xsn = _fnet_in_call(h, w_f, cs_chan)
        ysgu = _sgu_call(h, w_s, g_sgu[l], w_sp[l].astype(BF16), b_sp[l].T)

        y_f, hfin_f = _ssd_call(xbc, dt, state_ssd, l, e_mats[0], alogx[0], alog128, geom,
                                reverse=False, final=False)
        yssd, hfin_b = _ssd_call(xbc, dt, state_ssd, l, e_mats[1], alogx[1], alog128, geom,
                                 reverse=True, final=True,
                                 extra=(y_f, zs, dskx, g_ssd[l].reshape(1, hp)))
        states.append(jnp.stack([hfin_f[:batch], hfin_b[:batch]], axis=1))

        yfnet = _fnet_pos_call(xc, xsn, cc, sc, cl, sl, ctx_rows)
        x, h = _mixout_call(yssd, yfnet, ysgu, x, w_out[l].astype(BF16), mod, ng, dec_seq)

        last = l == depth - 1
        nxt = l if last else l + 1
        res = _ffn_call(h, x, w_ffn2_up[l].astype(BF16), w_ffn2_down[l].astype(BF16), mod, ng,
                        mod_all[nxt], norm_g[nxt], dec_seq, gate=8, ng_out=5, ng_next=0,
                        sh_next=0, sc_next=1, emit_next=not last)
        if last:
            x = res[0] if isinstance(res, (list, tuple)) else res
        else:
            x, h = res

    y_prompt = x[:ctx_rows].reshape(batch, seq, d)
    y_sample = x[ctx_rows:].reshape(dec_batch, dec_seq, d)
    new_state = jnp.stack(states, axis=1).reshape(
        batch, depth, 2, n_heads, SSD_HEAD_DIM, SSD_STATE).astype(x_prompt.dtype)
    return (y_prompt, y_sample, new_state)
```

```python
import functools
import math

import numpy as np
import jax
import jax.numpy as jnp
from jax import lax
from jax.experimental import pallas as pl
from jax.experimental.pallas import tpu as pltpu

F32 = jnp.float32
BF16 = jnp.bfloat16
EPS = 1e-6

LANES = 128
VMEM_LIMIT = 62 * 1024 * 1024

SSD_HEAD_DIM = 64
SSD_GROUPS = 4
SSD_STATE = 128
CHUNK = 128
CONV_K = 5
GRID_W = 64
FNET_GROUPS = 4
SGU_HEADS = 8
N_MOD = 9


def _cparams(sem):
    return pltpu.CompilerParams(dimension_semantics=sem, vmem_limit_bytes=VMEM_LIMIT)


def _rms(x, g):
    ms = jnp.mean(x * x, axis=-1, keepdims=True)
    return x * lax.rsqrt(ms + EPS) * g


def _silu(x):
    return x * (1.0 / (1.0 + jnp.exp(-x)))


def _split3(v):
    hi = v.astype(BF16)
    r1 = v - hi.astype(F32)
    mid = r1.astype(BF16)
    lo = (r1 - mid.astype(F32)).astype(BF16)
    return hi, mid, lo


def _dot(a, b):
    return jnp.dot(a, b, preferred_element_type=F32)


def _dot_exact_rhs(v, m_bf):
    hi, mid, lo = _split3(v)
    return _dot(hi, m_bf) + _dot(mid, m_bf) + _dot(lo, m_bf)


def _dot_exact_lhs(m_bf, v):
    hi, mid, lo = _split3(v)
    return _dot(m_bf, hi) + _dot(m_bf, mid) + _dot(m_bf, lo)


def _mod_kernel(c_ref, w_ref, b_ref, o_ref):
    k = pl.program_id(1)
    s = _silu(c_ref[...]).astype(BF16)
    part = _dot(s, w_ref[0].astype(BF16))

    @pl.when(k == 0)
    def _():
        o_ref[0] = part + b_ref[0]

    @pl.when(k > 0)
    def _():
        o_ref[0] += part


def _mod_call(cvec, w_mod, b_mod):
    depth, d, n = w_mod.shape
    kc = 256
    return pl.pallas_call(
        _mod_kernel,
        grid=(depth, d // kc),
        in_specs=[
            pl.BlockSpec((8, kc), lambda l, k: (0, k)),
            pl.BlockSpec((1, kc, n), lambda l, k: (l, k, 0)),
            pl.BlockSpec((1, 1, n), lambda l, k: (l, 0, 0)),
        ],
        out_specs=pl.BlockSpec((1, 8, n), lambda l, k: (l, 0, 0)),
        out_shape=jax.ShapeDtypeStruct((depth, 8, n), F32),
        compiler_params=_cparams(("arbitrary", "arbitrary")),
        name="mod",
    )(cvec, w_mod, b_mod.reshape(depth, 1, n))


def _prenorm_kernel(x_ref, mod_ref, ng_ref, h_ref):
    h = _rms(x_ref[...], ng_ref[0:1, :]) * (1.0 + mod_ref[0, 1:2, :]) + mod_ref[0, 0:1, :]
    h_ref[...] = h.astype(BF16)


def _prenorm_call(x, mod, ng, rows_per_group):
    m, d = x.shape
    tm = 512
    return pl.pallas_call(
        _prenorm_kernel,
        grid=(m // tm,),
        in_specs=[
            pl.BlockSpec((tm, d), lambda i: (i, 0)),
            pl.BlockSpec((1, N_MOD, d), lambda i: (i * tm // rows_per_group, 0, 0)),
            pl.BlockSpec(ng.shape, lambda i: (0, 0)),
        ],
        out_specs=pl.BlockSpec((tm, d), lambda i: (i, 0)),
        out_shape=jax.ShapeDtypeStruct((m, d), BF16),
        compiler_params=_cparams(("arbitrary",)),
        name="prenorm",
    )(x, mod, ng)


def _ffn_kernel(h_ref, x_ref, wg_ref, wu_ref, wd_ref, mod_ref, ng_ref, modn_ref, ngn_ref,
                *rest, gate, ng_out, ng_next, sh_next, sc_next, emit_next, split_tile):
    if emit_next:
        xo_ref, ho_ref, acc_ref, act_ref = rest
        xo_refs = (xo_ref,)
    else:
        xo_a_ref, xo_b_ref, acc_ref, act_ref = rest
        xo_refs = (xo_a_ref, xo_b_ref)
    i = pl.program_id(0)
    j = pl.program_id(1)
    nf = pl.num_programs(1) - 2
    tm, d_model = acc_ref.shape
    th = tm // 2
    n_split = 4
    cw = d_model // n_split

    @pl.when(j == 0)
    def _():
        act_ref[1] = jnp.zeros(act_ref.shape[1:], BF16)
        acc_ref[...] = jnp.zeros(acc_ref.shape, F32)

    @pl.when(j < nf)
    def _():
        h = h_ref[...]
        g = _dot(h, wg_ref[...])
        u = _dot(h, wu_ref[...])
        a_prev = act_ref[(j + 1) % 2]
        for cblk in range(n_split):
            cols = slice(cblk * cw, (cblk + 1) * cw)
            acc_ref[:, cols] += _dot(a_prev, wd_ref[:, cols])
        act_ref[j % 2] = (_silu(g) * u).astype(BF16)

    def finish(y):
        xn = x_ref[...] + 0.5 * mod_ref[0, gate:gate + 1, :] * _rms(y, ng_ref[ng_out:ng_out + 1, :])
        if emit_next:
            xo_ref[...] = xn
            hn = (_rms(xn, ngn_ref[ng_next:ng_next + 1, :])
                  * (1.0 + modn_ref[0, sc_next:sc_next + 1, :]) + modn_ref[0, sh_next:sh_next + 1, :])
            ho_ref[...] = hn.astype(BF16)
        else:
            @pl.when(i < split_tile)
            def _():
                xo_refs[0][...] = xn

            @pl.when(i >= split_tile)
            def _():
                xo_refs[1][...] = xn

    @pl.when(j == nf)
    def _():
        a_prev = act_ref[(j + 1) % 2]
        acc_ref[th:, :] += _dot(a_prev[th:, :], wd_ref[...])
        finish(acc_ref[:th, :] + _dot(a_prev[:th, :], wd_ref[...]))

    @pl.when(j == nf + 1)
    def _():
        finish(acc_ref[th:, :])


def _ffn_call(h, x, w_up, w_down, mod, ng, mod_next, ng_next_arr, rows_per_group, *, gate, ng_out,
              ng_next, sh_next, sc_next, emit_next, split_rows=0):
    m, d = x.shape
    f = w_down.shape[0]
    tm, fc = 1024, 512
    th = tm // 2
    nf = f // fc
    kern = functools.partial(_ffn_kernel, gate=gate, ng_out=ng_out, ng_next=ng_next,
                             sh_next=sh_next, sc_next=sc_next, emit_next=emit_next,
                             split_tile=split_rows // tm)
    half = lambda i, j: 2 * i + jnp.where(j > nf, 1, 0)
    row_h = pl.BlockSpec((tm, d), lambda i, j: (i, 0))
    row_x = pl.BlockSpec((th, d), lambda i, j: (half(i, j), 0))
    modspec = pl.BlockSpec((1, N_MOD, d), lambda i, j: (i * tm // rows_per_group, 0, 0))
    ngspec = pl.BlockSpec(ng.shape, lambda i, j: (0, 0))
    if emit_next:
        out_specs = [row_x, row_x]
        out_shape = [jax.ShapeDtypeStruct((m, d), F32), jax.ShapeDtypeStruct((m, d), BF16)]
    else:
        assert split_rows % tm == 0
        na = split_rows // th
        out_specs = [pl.BlockSpec((th, d), lambda i, j: (jnp.minimum(half(i, j), na - 1), 0)),
                     pl.BlockSpec((th, d), lambda i, j: (jnp.maximum(half(i, j) - na, 0), 0))]
        out_shape = [jax.ShapeDtypeStruct((split_rows, d), F32),
                     jax.ShapeDtypeStruct((m - split_rows, d), F32)]
    return pl.pallas_call(
        kern,
        grid=(m // tm, nf + 2),
        in_specs=[
            row_h, row_x,
            pl.BlockSpec((d, fc), lambda i, j: (0, jnp.minimum(j, nf - 1))),
            pl.BlockSpec((d, fc), lambda i, j: (0, jnp.minimum(j, nf - 1) + nf)),
            pl.BlockSpec((fc, d), lambda i, j: (jnp.clip(j - 1, 0, nf - 1), 0)),
            modspec, ngspec, modspec, ngspec,
        ],
        out_specs=out_specs,
        out_shape=out_shape,
        scratch_shapes=[pltpu.VMEM((tm, d), F32), pltpu.VMEM((2, tm, fc), BF16)],
        compiler_params=_cparams(("arbitrary", "arbitrary")),
        name="ffn",
    )(h, x, w_up, w_up, w_down, mod, ng, mod_next, ng_next_arr)


def _z_kernel(h_ref, w_ref, o_ref):
    o_ref[...] = _silu(_dot(h_ref[...], w_ref[...])).astype(BF16)


def _z_call(h, w):
    m, d = h.shape
    n = w.shape[1]
    tm, tn = 1024, 1024
    return pl.pallas_call(
        _z_kernel,
        grid=(m // tm, n // tn),
        in_specs=[pl.BlockSpec((tm, d), lambda i, j: (i, 0)),
                  pl.BlockSpec((d, tn), lambda i, j: (0, j))],
        out_specs=pl.BlockSpec((tm, tn), lambda i, j: (i, j)),
        out_shape=jax.ShapeDtypeStruct((m, n), BF16),
        compiler_params=_cparams(("arbitrary", "arbitrary")),
        name="inproj_z",
    )(h, w)


HALO = 8


def _xbc_kernel(h_ref, w_ref, cw_ref, cb_ref, o_ref, s_ref, *, ctx_rows, ctx_seg, lat_seg):
    tm, tn = o_ref.shape
    i = pl.program_id(0)
    half = CONV_K // 2
    cwid = 2 * LANES
    per = ctx_seg // lat_seg
    is_lat = i * tm >= ctx_rows
    rows8 = lax.broadcasted_iota(jnp.int32, (HALO, cwid), 0)

    @pl.when((i == 0) & (pl.program_id(1) == 0))
    def _():
        s_ref[0:HALO, :] = jnp.zeros((HALO, tn), F32)
        s_ref[HALO + tm:2 * HALO + tm, :] = jnp.zeros((HALO, tn), F32)

    def pad_select(v, crosses, static_boundary):
        if static_boundary:
            return jnp.where(crosses, 0.0, v)
        return jnp.where(jnp.logical_and(crosses, is_lat), 0.0, v)

    for nb in range(tn // cwid):
        cols = slice(nb * cwid, (nb + 1) * cwid)
        s_ref[HALO:HALO + tm, cols] = _dot(h_ref[...], w_ref[:, cols])
        for g in range(tm // lat_seg):
            r0 = HALO + g * lat_seg
            pieces = []
            for p0, plen in ((0, HALO), (HALO, lat_seg - 2 * HALO), (lat_seg - HALO, HALO)):
                out = cb_ref[:, cols]
                for dlt in range(-half, half + 1):
                    v = s_ref[r0 + p0 + dlt:r0 + p0 + dlt + plen, cols]
                    if p0 == 0 and dlt < 0:
                        v = pad_select(v, rows8 < -dlt, g % per == 0)
                    if p0 == lat_seg - HALO and dlt > 0:
                        v = pad_select(v, rows8 >= HALO - dlt, g % per == per - 1)
                    out = out + v * cw_ref[half + dlt:half + dlt + 1, cols]
                pieces.append(out)
            seg_out = jnp.concatenate(pieces, axis=0)
            o_ref[g * lat_seg:(g + 1) * lat_seg, cols] = _silu(seg_out).astype(BF16)


def _xbc_call(h, w, conv_w, conv_b, ctx_rows, ctx_seg):
    m, d = h.shape
    n = w.shape[1]
    tm, tn = 512, 1024
    assert tm % ctx_seg == 0 and ctx_seg % GRID_W == 0 and ctx_rows % tm == 0
    assert GRID_W >= 2 * HALO and CONV_K // 2 <= HALO
    kern = functools.partial(_xbc_kernel, ctx_rows=ctx_rows, ctx_seg=ctx_seg, lat_seg=GRID_W)
    stage_rows = tm + 2 * HALO
    return pl.pallas_call(
        kern,
        grid=(m // tm, n // tn),
        in_specs=[pl.BlockSpec((tm, d), lambda i, j: (i, 0)),
                  pl.BlockSpec((d, tn), lambda i, j: (0, j)),
                  pl.BlockSpec((CONV_K, tn), lambda i, j: (0, j)),
                  pl.BlockSpec((1, tn), lambda i, j: (0, j))],
        out_specs=pl.BlockSpec((tm, tn), lambda i, j: (i, j)),
        out_shape=jax.ShapeDtypeStruct((m, n), BF16),
        scratch_shapes=[pltpu.VMEM((stage_rows, tn), F32)],
        compiler_params=_cparams(("arbitrary", "arbitrary")),
        name="inproj_xbc",
    )(h, w, conv_w, conv_b.reshape(1, n))


def _dt_kernel(h_ref, w_ref, b_ref, o_ref):
    v = _dot(h_ref[...], w_ref[...]) + b_ref[...]
    o_ref[...] = jnp.maximum(v, 0.0) + jnp.log1p(jnp.exp(-jnp.abs(v)))


def _dt_call(h, w, b):
    m, d = h.shape
    n = w.shape[1]
    tm = 1024
    return pl.pallas_call(
        _dt_kernel,
        grid=(m // tm,),
        in_specs=[pl.BlockSpec((tm, d), lambda i: (i, 0)),
                  pl.BlockSpec((d, n), lambda i: (0, 0)),
                  pl.BlockSpec((1, n), lambda i: (0, 0))],
        out_specs=pl.BlockSpec((tm, n), lambda i: (i, 0)),
        out_shape=jax.ShapeDtypeStruct((m, n), F32),
        compiler_params=_cparams(("arbitrary",)),
        name="inproj_dt",
    )(h, w, b)


def _fnet_in_kernel(h_ref, w_ref, cs_ref, xc_ref, xs_ref):
    f = _dot(h_ref[...], w_ref[...]).astype(BF16)
    gw = cs_ref.shape[0]
    for g in range(FNET_GROUPS):
        sl = slice(g * gw, (g + 1) * gw)
        r = _dot(f[:, sl], cs_ref[...])
        xc_ref[:, sl] = r[:, :gw].astype(BF16)
        xs_ref[:, sl] = r[:, gw:].astype(BF16)


def _fnet_in_call(h, w, cs):
    m, d = h.shape
    n = w.shape[1]
    tm = 512
    spec_o = pl.BlockSpec((tm, n), lambda i: (i, 0))
    return pl.pallas_call(
        _fnet_in_kernel,
        grid=(m // tm,),
        in_specs=[pl.BlockSpec((tm, d), lambda i: (i, 0)),
                  pl.BlockSpec((d, n), lambda i: (0, 0)),
                  pl.BlockSpec(cs.shape, lambda i: (0, 0))],
        out_specs=[spec_o, spec_o],
        out_shape=[jax.ShapeDtypeStruct((m, n), BF16)] * 2,
        compiler_params=_cparams(("arbitrary",)),
        name="inproj_fnet",
    )(h, w, cs)


def _fnet_pos_kernel(xc_c_ref, xs_c_ref, xc_l_ref, xs_l_ref, cc_ref, sc_ref, cl_ref, sl_ref, o_ref,
                     *, n_ctx_tiles):
    i = pl.program_id(0)

    @pl.when(i < n_ctx_tiles)
    def _():
        o_ref[...] = (_dot(cc_ref[...], xc_c_ref[...]) - _dot(sc_ref[...], xs_c_ref[...])).astype(BF16)

    @pl.when(i >= n_ctx_tiles)
    def _():
        o_ref[...] = (_dot(cl_ref[...], xc_l_ref[...]) - _dot(sl_ref[...], xs_l_ref[...])).astype(BF16)


def _fnet_pos_call(xc, xs, cc, sc, cl, sl, ctx_rows):
    m, n = xc.shape
    tm = cc.shape[0]
    lat_len = cl.shape[0]
    n_ctx = ctx_rows // tm
    per_lat = lat_len // tm
    ctx_spec = pl.BlockSpec((tm, n), lambda i: (jnp.minimum(i, n_ctx - 1), 0))
    lat_spec = pl.BlockSpec(
        (lat_len, n), lambda i: (ctx_rows // lat_len + jnp.maximum(i - n_ctx, 0) // per_lat, 0))
    dft_c = pl.BlockSpec((tm, tm), lambda i: (0, 0))
    dft_l = pl.BlockSpec((tm, lat_len), lambda i: (jnp.maximum(i - n_ctx, 0) % per_lat, 0))
    kern = functools.partial(_fnet_pos_kernel, n_ctx_tiles=n_ctx)
    return pl.pallas_call(
        kern,
        grid=(m // tm,),
        in_specs=[ctx_spec, ctx_spec, lat_spec, lat_spec, dft_c, dft_c, dft_l, dft_l],
        out_specs=pl.BlockSpec((tm, n), lambda i: (i, 0)),
        out_shape=jax.ShapeDtypeStruct((m, n), BF16),
        compiler_params=_cparams(("arbitrary",)),
        name="fnet_pos",
    )(xc, xs, xc, xs, cc, sc, cl, sl)


def _sgu_kernel(h_ref, w_ref, g_ref, wsp_ref, bsp_ref, o_ref):
    tm, width = o_ref.shape
    hd = width // SGU_HEADS
    acc = _dot(h_ref[...], w_ref[...])
    k0 = math.sqrt(2.0 / math.pi)
    s = 0.5 * acc * (1.0 + jnp.tanh(k0 * (acc + 0.044715 * (acc * acc * acc))))
    u = s[:, :width]
    v = _rms(s[:, width:], g_ref[...]).astype(BF16)
    for k in range(tm // CHUNK):
        rows = slice(k * CHUNK, (k + 1) * CHUNK)
        for hh in range(SGU_HEADS):
            cols = slice(hh * hd, (hh + 1) * hd)
            sp = _dot(wsp_ref[hh], v[rows, cols]) + bsp_ref[:, hh:hh + 1]
            o_ref[rows, cols] = (u[rows, cols] * sp).astype(BF16)


def _sgu_call(h, w, g_sgu, w_sp, b_sp_t):
    m, d = h.shape
    n = w.shape[1]
    width = n // 2
    tm = 512
    return pl.pallas_call(
        _sgu_kernel,
        grid=(m // tm,),
        in_specs=[pl.BlockSpec((tm, d), lambda i: (i, 0)),
                  pl.BlockSpec((d, n), lambda i: (0, 0)),
                  pl.BlockSpec((1, width), lambda i: (0, 0)),
                  pl.BlockSpec(w_sp.shape, lambda i: (0, 0, 0)),
                  pl.BlockSpec(b_sp_t.shape, lambda i: (0, 0))],
        out_specs=pl.BlockSpec((tm, width), lambda i: (i, 0)),
        out_shape=jax.ShapeDtypeStruct((m, width), BF16),
        compiler_params=_cparams(("arbitrary",)),
        name="inproj_sgu",
    )(h, w, g_sgu.reshape(1, width), w_sp, b_sp_t)


def _ssd_kernel(tbl_ref, xs_ref, b_ref, c_ref, dt_ref, h0_ref, e_ref, alogx_ref, alog_ref, *rest,
                reverse, final, n_ctx_seq, lane_off):
    if final:
        yprev_ref, zs_ref, dsk_ref, gssd_ref, y_ref, hfin_ref, st_ref, ybuf_ref = rest
    else:
        y_ref, hfin_ref, st_ref = rest
    s = pl.program_id(0)
    seq = tbl_ref[1, s]
    first = tbl_ref[2, s]
    last = tbl_ref[3, s]
    T = CHUNK
    hp = xs_ref.shape[1]
    gw = hp // SSD_GROUPS
    n_heads = hp // SSD_HEAD_DIM

    @pl.when(first == 1)
    def _():
        h0 = h0_ref[0, 0, 0].reshape(hp, SSD_STATE)
        h0 = jnp.where(seq < n_ctx_seq, 0.0, h0)
        st_ref[...] = h0.T

    row = lax.broadcasted_iota(jnp.int32, (T, T), 0)
    col = lax.broadcasted_iota(jnp.int32, (T, T), 1)
    tri = (row <= col) if reverse else (row >= col)
    tri_bf = jnp.where(tri, 1.0, 0.0).astype(BF16)

    dt = dt_ref[...]
    dtx = _dot_exact_rhs(dt, e_ref[...])
    ax = dtx * (-jnp.exp(alogx_ref[...]))
    acsx = _dot_exact_lhs(tri_bf, ax)
    acs = _dot_exact_lhs(tri_bf, dt * (-jnp.exp(alog_ref[...])))
    acs_t = acs.T

    xs = xs_ref[...].astype(F32)
    xp = xs * dtx
    xp_bf = xp.astype(BF16)
    end = acsx[0:1, :] if reverse else acsx[T - 1:T, :]
    xpd = (xp * jnp.exp(end - acsx)).astype(BF16)
    ex = jnp.exp(acsx)
    dec = jnp.exp(end)
    st = st_ref[...]
    st_bf = st.astype(BF16)
    lane = lax.broadcasted_iota(jnp.int32, (T, LANES), 1)
    low_half = lane < SSD_HEAD_DIM
    if final:
        dsum = dsk_ref[0:1, :] + dsk_ref[1:2, :]
        ssq = jnp.zeros((T, 1), F32)

    heads_per_group = n_heads // SSD_GROUPS
    for g in range(SSD_GROUPS):
        bg = b_ref[:, g * SSD_STATE:(g + 1) * SSD_STATE]
        cg = c_ref[:, g * SSD_STATE:(g + 1) * SSD_STATE]
        gs = slice(g * gw, (g + 1) * gw)
        cb = lax.dot_general(cg, bg, (((1,), (1,)), ((), ())), preferred_element_type=F32)
        yoff = _dot(cg, st_bf[:, gs])
        stg = lax.dot_general(bg, xpd[:, gs], (((0,), (0,)), ((), ())), preferred_element_type=F32)
        st_ref[:, gs] = st[:, gs] * dec[:, gs] + stg
        for q in range(heads_per_group // 2):
            ms = []
            for r in (2 * q, 2 * q + 1):
                hl = lane_off + g * heads_per_group + r
                diff = acs[:, hl:hl + 1] - acs_t[hl:hl + 1, :]
                lm = jnp.exp(jnp.where(tri, diff, -jnp.inf))
                ms.append((cb * lm).astype(BF16))
            c0 = g * gw + q * LANES
            cs_ = slice(c0, c0 + LANES)
            xpair = xp_bf[:, cs_]
            zero = jnp.zeros_like(xpair)
            rhs = jnp.concatenate([jnp.where(low_half, xpair, zero), jnp.where(low_half, zero, xpair)],
                                  axis=0)
            yd = _dot(jnp.concatenate(ms, axis=1), rhs)
            yblk = yd + yoff[:, q * LANES:(q + 1) * LANES] * ex[:, cs_]
            if final:
                t = (yblk + yprev_ref[:, cs_] + xs[:, cs_] * dsum[:, cs_]) * zs_ref[:, cs_].astype(F32)
                ssq = ssq + jnp.sum(t * t, axis=-1, keepdims=True)
                ybuf_ref[:, cs_] = t
            else:
                y_ref[:, cs_] = yblk

    if final:
        scale = lax.rsqrt(ssq / hp + EPS)
        y_ref[...] = (ybuf_ref[...] * scale * gssd_ref[...]).astype(BF16)

    @pl.when(last == 1)
    def _():
        hfin_ref[0] = st_ref[...].T


def _ssd_tables(n_ctx_seq, ctx_chunks, n_lat_seq, lat_chunks, reverse):
    blk, seq, first, last = [], [], [], []
    base = 0
    for sidx, nch in [(i, ctx_chunks) for i in range(n_ctx_seq)] + \
                     [(n_ctx_seq + i, lat_chunks) for i in range(n_lat_seq)]:
        order = list(range(nch))[::-1] if reverse else list(range(nch))
        for k, cidx in enumerate(order):
            blk.append(base + cidx)
            seq.append(sidx)
            first.append(1 if k == 0 else 0)
            last.append(1 if k == nch - 1 else 0)
        base += nch
    return np.array([blk, seq, first, last], dtype=np.int32)


def _ssd_call(xbc, dt, state_ssd, layer, e_mat, alogx, alog128, geom, *, reverse, final, extra=()):
    n_ctx_seq, ctx_chunks, n_lat_seq, lat_chunks = geom
    m = xbc.shape[0]
    hp = e_mat.shape[1]
    n_groups_cols = SSD_GROUPS * SSD_STATE
    tbl = jnp.asarray(_ssd_tables(n_ctx_seq, ctx_chunks, n_lat_seq, lat_chunks, reverse))
    steps = tbl.shape[1]
    n_seq = n_ctx_seq + n_lat_seq
    direction = 1 if reverse else 0
    n_heads = hp // SSD_HEAD_DIM
    kern = functools.partial(_ssd_kernel, reverse=reverse, final=final, n_ctx_seq=n_ctx_seq,
                             lane_off=direction * n_heads)
    rowblk = lambda width, cblk: pl.BlockSpec((CHUNK, width), lambda s, t: (t[0, s], cblk))
    const2 = lambda shape: pl.BlockSpec(shape, lambda s, t: (0, 0))
    in_specs = [
        rowblk(hp, 0),
        rowblk(n_groups_cols, hp // n_groups_cols),
        rowblk(n_groups_cols, hp // n_groups_cols + 1),
        rowblk(LANES, 0),
        pl.BlockSpec((1, 1, 1) + state_ssd.shape[3:],
                     lambda s, t: (jnp.maximum(t[1, s] - n_ctx_seq, 0), layer, direction, 0, 0, 0)),
        const2(e_mat.shape), const2(alogx.shape), const2(alog128.shape),
    ]
    args = [xbc, xbc, xbc, dt, state_ssd, e_mat, alogx, alog128]
    scratch = [pltpu.VMEM((SSD_STATE, hp), F32)]
    if final:
        yprev, zs, dsk, gssd = extra
        in_specs += [rowblk(hp, 0), rowblk(hp, 0), const2(dsk.shape), const2(gssd.shape)]
        args += [yprev, zs, dsk, gssd]
        scratch.append(pltpu.VMEM((CHUNK, hp), F32))
    y_dtype = BF16 if final else F32
    grid_spec = pltpu.PrefetchScalarGridSpec(
        num_scalar_prefetch=1,
        grid=(steps,),
        in_specs=in_specs,
        out_specs=[rowblk(hp, 0),
                   pl.BlockSpec((1, hp, SSD_STATE), lambda s, t: (t[1, s], 0, 0))],
        scratch_shapes=scratch,
    )
    return pl.pallas_call(
        kern,
        grid_spec=grid_spec,
        out_shape=[jax.ShapeDtypeStruct((m, hp), y_dtype),
                   jax.ShapeDtypeStruct((n_seq, hp, SSD_STATE), F32)],
        compiler_params=_cparams(("arbitrary",)),
        name="ssd_bwd" if reverse else "ssd_fwd",
    )(tbl, *args)


def _mixout_kernel(ys_ref, yf_ref, yg_ref, x_ref, w_ref, mod_ref, ng_ref, xo_ref, ho_ref):
    ks = ys_ref.shape[1]
    kf = yf_ref.shape[1]
    mm = (_dot(ys_ref[...], w_ref[:ks, :]) + _dot(yf_ref[...], w_ref[ks:ks + kf, :])
          + _dot(yg_ref[...], w_ref[ks + kf:, :]))
    xn = x_ref[...] + mod_ref[0, 5:6, :] * _rms(mm, ng_ref[3:4, :])
    xo_ref[...] = xn
    hn = _rms(xn, ng_ref[4:5, :]) * (1.0 + mod_ref[0, 7:8, :]) + mod_ref[0, 6:7, :]
    ho_ref[...] = hn.astype(BF16)


def _mixout_call(yssd, yfnet, ysgu, x, w_out, mod, ng, rows_per_group):
    m, d = x.shape
    assert yssd.shape[1] + yfnet.shape[1] + ysgu.shape[1] == w_out.shape[0]
    tm = 512
    row = lambda width: pl.BlockSpec((tm, width), lambda i: (i, 0))
    return pl.pallas_call(
        _mixout_kernel,
        grid=(m // tm,),
        in_specs=[row(yssd.shape[1]), row(yfnet.shape[1]), row(ysgu.shape[1]), row(d),
                  pl.BlockSpec(w_out.shape, lambda i: (0, 0), pipeline_mode=pl.Buffered(1)),
                  pl.BlockSpec((1, N_MOD, d), lambda i: (i * tm // rows_per_group, 0, 0)),
                  pl.BlockSpec(ng.shape, lambda i: (0, 0))],
        out_specs=[row(d), row(d)],
        out_shape=[jax.ShapeDtypeStruct((m, d), F32), jax.ShapeDtypeStruct((m, d), BF16)],
        compiler_params=_cparams(("arbitrary",)),
        name="mixout",
    )(yssd, yfnet, ysgu, x, w_out, mod, ng)


def _dft_tables(n):
    scale = 1.0 / math.sqrt(n)
    r = int(round(math.sqrt(n)))
    if r * r != n or n < 1024:
        j = lax.broadcasted_iota(jnp.int32, (n, n), 0)
        k = lax.broadcasted_iota(jnp.int32, (n, n), 1)
        ang = ((j * k) % n).astype(F32) * (2.0 * math.pi / n)
        return (jnp.cos(ang) * scale).astype(BF16), (jnp.sin(ang) * scale).astype(BF16)
    jj = lax.broadcasted_iota(jnp.int32, (r, n), 0)
    kk = lax.broadcasted_iota(jnp.int32, (r, n), 1)
    a1 = ((jj * r * kk) % n).astype(F32) * (2.0 * math.pi / n)
    a2 = ((jj * kk) % n).astype(F32) * (2.0 * math.pi / n)
    c1, s1 = (jnp.cos(a1) * scale)[:, None, :], (jnp.sin(a1) * scale)[:, None, :]
    c2, s2 = jnp.cos(a2)[None, :, :], jnp.sin(a2)[None, :, :]
    cos = (c1 * c2 - s1 * s2).reshape(n, n)
    sin = (s1 * c2 + c1 * s2).reshape(n, n)
    return cos.astype(BF16), sin.astype(BF16)


def kernel(x_prompt, x_sample, state_ssd, c, c_ctx, w_mod, b_mod, norm_g, w_ffn1_up, w_ffn1_down,
           w_in, conv_w, conv_b, a_log, dt_bias, d_skip, g_ssd, g_sgu, w_sp, b_sp, w_out,
           w_ffn2_up, w_ffn2_down):
    batch, seq, d = x_prompt.shape
    dec_batch, dec_seq, _ = x_sample.shape
    depth = w_mod.shape[0]
    ctx_rows = batch * seq
    assert ctx_rows == dec_seq, "row tiles map to modulation groups in units of dec_seq rows"
    n_heads = a_log.shape[2]
    hp = n_heads * SSD_HEAD_DIM
    gn = SSD_GROUPS * SSD_STATE
    off_dt = hp + hp + 2 * gn
    off_fnet = off_dt + 2 * n_heads
    fnet_w = w_out.shape[1] // 4
    off_sgu = off_fnet + fnet_w

    x = jnp.concatenate([x_prompt.reshape(ctx_rows, d), x_sample.reshape(dec_batch * dec_seq, d)], axis=0)
    cvec = jnp.concatenate([c_ctx[None, :], c, jnp.zeros((8 - 1 - dec_batch, d), F32)], axis=0)
    mod_all = _mod_call(cvec, w_mod, b_mod).reshape(depth, 8, N_MOD, d)

    gw = fnet_w // FNET_GROUPS
    cc_g, sc_g = _dft_tables(gw)
    cs_chan = jnp.concatenate([cc_g, sc_g], axis=1)
    cc, sc = _dft_tables(seq)
    cl, sl = _dft_tables(dec_seq)
    e_np = np.zeros((2, LANES, hp), np.float32)
    for dr in range(2):
        for hh in range(n_heads):
            e_np[dr, dr * n_heads + hh, hh * SSD_HEAD_DIM:(hh + 1) * SSD_HEAD_DIM] = 1.0
    e_mats = [jnp.asarray(e_np[0], BF16), jnp.asarray(e_np[1], BF16)]
    geom = (batch, seq // CHUNK, dec_batch, dec_seq // CHUNK)

    h = _prenorm_call(x, mod_all[0], norm_g[0], dec_seq)
    states = []
    for l in range(depth):
        mod = mod_all[l]
        ng = norm_g[l]
        w_in_l = w_in[l]
        w_z = w_in_l[:, :hp].astype(BF16)
        w_xbc = w_in_l[:, hp:off_dt].astype(BF16)
        w_dt = jnp.pad(w_in_l[:, off_dt:off_fnet], ((0, 0), (0, LANES - 2 * n_heads))).astype(BF16)
        w_f = w_in_l[:, off_fnet:off_sgu].astype(BF16)
        w_s = w_in_l[:, off_sgu:].astype(BF16)
        dtb = jnp.pad(dt_bias[l].reshape(1, 2 * n_heads), ((0, 0), (0, LANES - 2 * n_heads)))
        alog128 = jnp.pad(a_log[l].reshape(1, 2 * n_heads), ((0, 0), (0, LANES - 2 * n_heads)))
        alogx = [jnp.repeat(a_log[l, dr], SSD_HEAD_DIM).reshape(1, hp) for dr in range(2)]
        dskx = jnp.repeat(d_skip[l], SSD_HEAD_DIM, axis=1)

        x, h = _ffn_call(h, x, w_ffn1_up[l].astype(BF16), w_ffn1_down[l].astype(BF16), mod, ng, mod, ng,
                         dec_seq, gate=2, ng_out=1, ng_next=2, sh_next=3, sc_next=4, emit_next=True)

        zs = _z_call(h, w_z)
        xbc = _xbc_call(h, w_xbc, conv_w[l], conv_b[l], ctx_rows, seq)
        dt = _dt_call(h, w_dt, dtb)
        xc, xsn = _fnet_in_call(h, w_f, cs_chan)
        ysgu = _sgu_call(h, w_s, g_sgu[l], w_sp[l].astype(BF16), b_sp[l].T)

        y_f, hfin_f = _ssd_call(xbc, dt, state_ssd, l, e_mats[0], alogx[0], alog128, geom,
                                reverse=False, final=False)
        yssd, hfin_b = _ssd_call(xbc, dt, state_ssd, l, e_mats[1], alogx[1], alog128, geom,
                                 reverse=True, final=True,
                                 extra=(y_f, zs, dskx, g_ssd[l].reshape(1, hp)))
        states.append(jnp.stack([hfin_f[:batch], hfin_b[:batch]], axis=1))

        yfnet = _fnet_pos_call(xc, xsn, cc, sc, cl, sl, ctx_rows)
        x, h = _mixout_call(yssd, yfnet, ysgu, x, w_out[l].astype(BF16), mod, ng, dec_seq)

        last = l == depth - 1
        nxt = l if last else l + 1
        res = _ffn_call(h, x, w_ffn2_up[l].astype(BF16), w_ffn2_down[l].astype(BF16), mod, ng,
                        mod_all[nxt], norm_g[nxt], dec_seq, gate=8, ng_out=5, ng_next=0,
                        sh_next=0, sc_next=1, emit_next=not last, split_rows=ctx_rows)
        if not last:
            x, h = res

    y_prompt = res[0].reshape(batch, seq, d)
    y_sample = res[1].reshape(dec_batch, dec_seq, d)
    new_state = jnp.stack(states, axis=1).reshape(
        batch, depth, 2, n_heads, SSD_HEAD_DIM, SSD_STATE).astype(x_prompt.dtype)
    return (y_prompt, y_sample, new_state)
```

```python
import functools
import math

import numpy as np
import jax
import jax.numpy as jnp
from jax import lax
from jax.experimental import pallas as pl
from jax.experimental.pallas import tpu as pltpu

F32 = jnp.float32
BF16 = jnp.bfloat16
EPS = 1e-6

LANES = 128
VMEM_LIMIT = 62 * 1024 * 1024

SSD_HEAD_DIM = 64
SSD_GROUPS = 4
SSD_STATE = 128
CHUNK = 128
CONV_K = 5
GRID_W = 64
FNET_GROUPS = 4
SGU_HEADS = 8
N_MOD = 9


def _cparams(sem):
    return pltpu.CompilerParams(dimension_semantics=sem, vmem_limit_bytes=VMEM_LIMIT)


def _rms(x, g):
    ms = jnp.mean(x * x, axis=-1, keepdims=True)
    return x * lax.rsqrt(ms + EPS) * g


def _silu(x):
    return x * (1.0 / (1.0 + jnp.exp(-x)))


def _split3(v):
    hi = v.astype(BF16)
    r1 = v - hi.astype(F32)
    mid = r1.astype(BF16)
    lo = (r1 - mid.astype(F32)).astype(BF16)
    return hi, mid, lo


def _dot(a, b):
    return jnp.dot(a, b, preferred_element_type=F32)


def _mod_kernel(c_ref, w_ref, b_ref, o_ref):
    k = pl.program_id(1)
    s = _silu(c_ref[...]).astype(BF16)
    part = _dot(s, w_ref[0].astype(BF16))

    @pl.when(k == 0)
    def _():
        o_ref[0] = part + b_ref[0]

    @pl.when(k > 0)
    def _():
        o_ref[0] += part


def _mod_call(cvec, w_mod, b_mod):
    depth, d, n = w_mod.shape
    kc = 256
    return pl.pallas_call(
        _mod_kernel,
        grid=(depth, d // kc),
        in_specs=[
            pl.BlockSpec((8, kc), lambda l, k: (0, k)),
            pl.BlockSpec((1, kc, n), lambda l, k: (l, k, 0)),
            pl.BlockSpec((1, 1, n), lambda l, k: (l, 0, 0)),
        ],
        out_specs=pl.BlockSpec((1, 8, n), lambda l, k: (l, 0, 0)),
        out_shape=jax.ShapeDtypeStruct((depth, 8, n), F32),
        compiler_params=_cparams(("arbitrary", "arbitrary")),
        name="mod",
    )(cvec, w_mod, b_mod.reshape(depth, 1, n))


def _prenorm_kernel(x_ref, mod_ref, ng_ref, h_ref):
    h = _rms(x_ref[...], ng_ref[0:1, :]) * (1.0 + mod_ref[0, 1:2, :]) + mod_ref[0, 0:1, :]
    h_ref[...] = h.astype(BF16)


def _prenorm_call(x, mod, ng, rows_per_group):
    m, d = x.shape
    tm = 512
    return pl.pallas_call(
        _prenorm_kernel,
        grid=(m // tm,),
        in_specs=[
            pl.BlockSpec((tm, d), lambda i: (i, 0)),
            pl.BlockSpec((1, N_MOD, d), lambda i: (i * tm // rows_per_group, 0, 0)),
            pl.BlockSpec(ng.shape, lambda i: (0, 0)),
        ],
        out_specs=pl.BlockSpec((tm, d), lambda i: (i, 0)),
        out_shape=jax.ShapeDtypeStruct((m, d), BF16),
        compiler_params=_cparams(("arbitrary",)),
        name="prenorm",
    )(x, mod, ng)


def _ffn_kernel(h_ref, x_ref, wg_ref, wu_ref, wd_ref, mod_ref, ng_ref, modn_ref, ngn_ref,
                *rest, gate, ng_out, ng_next, sh_next, sc_next, emit_next, split_tile):
    if emit_next:
        xo_ref, ho_ref, acc_ref, act_ref = rest
        xo_refs = (xo_ref,)
    else:
        xo_a_ref, xo_b_ref, acc_ref, act_ref = rest
        xo_refs = (xo_a_ref, xo_b_ref)
    i = pl.program_id(0)
    j = pl.program_id(1)
    nf = pl.num_programs(1) - 2
    tm, d_model = acc_ref.shape
    th = tm // 2
    n_split = 4
    cw = d_model // n_split

    @pl.when(j == 0)
    def _():
        h = h_ref[...]
        act_ref[0] = (_silu(_dot(h, wg_ref[...])) * _dot(h, wu_ref[...])).astype(BF16)
        acc_ref[...] = jnp.zeros(acc_ref.shape, F32)

    @pl.when((j > 0) & (j < nf))
    def _():
        h = h_ref[...]
        g = _dot(h, wg_ref[...])
        u = _dot(h, wu_ref[...])
        a_prev = act_ref[(j + 1) % 2]
        for cblk in range(n_split):
            cols = slice(cblk * cw, (cblk + 1) * cw)
            acc_ref[:, cols] += _dot(a_prev, wd_ref[:, cols])
        act_ref[j % 2] = (_silu(g) * u).astype(BF16)

    def finish(y):
        xn = x_ref[...] + 0.5 * mod_ref[0, gate:gate + 1, :] * _rms(y, ng_ref[ng_out:ng_out + 1, :])
        if emit_next:
            xo_ref[...] = xn
            hn = (_rms(xn, ngn_ref[ng_next:ng_next + 1, :])
                  * (1.0 + modn_ref[0, sc_next:sc_next + 1, :]) + modn_ref[0, sh_next:sh_next + 1, :])
            ho_ref[...] = hn.astype(BF16)
        else:
            @pl.when(i < split_tile)
            def _():
                xo_refs[0][...] = xn

            @pl.when(i >= split_tile)
            def _():
                xo_refs[1][...] = xn

    @pl.when(j == nf)
    def _():
        a_prev = act_ref[(j + 1) % 2]
        acc_ref[th:, :] += _dot(a_prev[th:, :], wd_ref[...])
        finish(acc_ref[:th, :] + _dot(a_prev[:th, :], wd_ref[...]))

    @pl.when(j == nf + 1)
    def _():
        finish(acc_ref[th:, :])


def _ffn_call(h, x, w_up, w_down, mod, ng, mod_next, ng_next_arr, rows_per_group, *, gate, ng_out,
              ng_next, sh_next, sc_next, emit_next, split_rows=0):
    m, d = x.shape
    f = w_down.shape[0]
    tm, fc = 1024, 512
    th = tm // 2
    nf = f // fc
    kern = functools.partial(_ffn_kernel, gate=gate, ng_out=ng_out, ng_next=ng_next,
                             sh_next=sh_next, sc_next=sc_next, emit_next=emit_next,
                             split_tile=split_rows // tm)
    half = lambda i, j: 2 * i + jnp.where(j > nf, 1, 0)
    row_h = pl.BlockSpec((tm, d), lambda i, j: (i, 0))
    row_x = pl.BlockSpec((th, d), lambda i, j: (half(i, j), 0))
    modspec = pl.BlockSpec((1, N_MOD, d), lambda i, j: (i * tm // rows_per_group, 0, 0))
    ngspec = pl.BlockSpec(ng.shape, lambda i, j: (0, 0))
    if emit_next:
        out_specs = [row_x, row_x]
        out_shape = [jax.ShapeDtypeStruct((m, d), F32), jax.ShapeDtypeStruct((m, d), BF16)]
    else:
        assert split_rows % tm == 0
        na = split_rows // th
        out_specs = [pl.BlockSpec((th, d), lambda i, j: (jnp.minimum(half(i, j), na - 1), 0)),
                     pl.BlockSpec((th, d), lambda i, j: (jnp.maximum(half(i, j) - na, 0), 0))]
        out_shape = [jax.ShapeDtypeStruct((split_rows, d), F32),
                     jax.ShapeDtypeStruct((m - split_rows, d), F32)]
    return pl.pallas_call(
        kern,
        grid=(m // tm, nf + 2),
        in_specs=[
            row_h, row_x,
            pl.BlockSpec((d, fc), lambda i, j: (0, jnp.minimum(j, nf - 1))),
            pl.BlockSpec((d, fc), lambda i, j: (0, jnp.minimum(j, nf - 1) + nf)),
            pl.BlockSpec((fc, d), lambda i, j: (jnp.clip(j - 1, 0, nf - 1), 0)),
            modspec, ngspec, modspec, ngspec,
        ],
        out_specs=out_specs,
        out_shape=out_shape,
        scratch_shapes=[pltpu.VMEM((tm, d), F32), pltpu.VMEM((2, tm, fc), BF16)],
        compiler_params=_cparams(("arbitrary", "arbitrary")),
        name="ffn",
    )(h, x, w_up, w_up, w_down, mod, ng, mod_next, ng_next_arr)


def _z_kernel(h_ref, w_ref, o_ref):
    o_ref[...] = _silu(_dot(h_ref[...], w_ref[...])).astype(BF16)


def _z_call(h, w):
    m, d = h.shape
    n = w.shape[1]
    tm, tn = 1024, 1024
    return pl.pallas_call(
        _z_kernel,
        grid=(m // tm, n // tn),
        in_specs=[pl.BlockSpec((tm, d), lambda i, j: (i, 0)),
                  pl.BlockSpec((d, tn), lambda i, j: (0, j))],
        out_specs=pl.BlockSpec((tm, tn), lambda i, j: (i, j)),
        out_shape=jax.ShapeDtypeStruct((m, n), BF16),
        compiler_params=_cparams(("arbitrary", "arbitrary")),
        name="inproj_z",
    )(h, w)


HALO = 8


def _xbc_kernel(h_ref, w_ref, cw_ref, cb_ref, o_ref, s_ref, *, ctx_rows, ctx_seg, lat_seg):
    tm, tn = o_ref.shape
    i = pl.program_id(0)
    half = CONV_K // 2
    cwid = 2 * LANES
    per = ctx_seg // lat_seg
    is_lat = i * tm >= ctx_rows
    rows8 = lax.broadcasted_iota(jnp.int32, (HALO, cwid), 0)

    @pl.when((i == 0) & (pl.program_id(1) == 0))
    def _():
        s_ref[0:HALO, :] = jnp.zeros((HALO, tn), F32)
        s_ref[HALO + tm:2 * HALO + tm, :] = jnp.zeros((HALO, tn), F32)

    def pad_select(v, crosses, static_boundary):
        if static_boundary:
            return jnp.where(crosses, 0.0, v)
        return jnp.where(jnp.logical_and(crosses, is_lat), 0.0, v)

    for nb in range(tn // cwid):
        cols = slice(nb * cwid, (nb + 1) * cwid)
        s_ref[HALO:HALO + tm, cols] = _dot(h_ref[...], w_ref[:, cols])
        for g in range(tm // lat_seg):
            r0 = HALO + g * lat_seg
            pieces = []
            for p0, plen in ((0, HALO), (HALO, lat_seg - 2 * HALO), (lat_seg - HALO, HALO)):
                out = cb_ref[:, cols]
                for dlt in range(-half, half + 1):
                    v = s_ref[r0 + p0 + dlt:r0 + p0 + dlt + plen, cols]
                    if p0 == 0 and dlt < 0:
                        v = pad_select(v, rows8 < -dlt, g % per == 0)
                    if p0 == lat_seg - HALO and dlt > 0:
                        v = pad_select(v, rows8 >= HALO - dlt, g % per == per - 1)
                    out = out + v * cw_ref[half + dlt:half + dlt + 1, cols]
                pieces.append(out)
            seg_out = jnp.concatenate(pieces, axis=0)
            o_ref[g * lat_seg:(g + 1) * lat_seg, cols] = _silu(seg_out).astype(BF16)


def _xbc_call(h, w, conv_w, conv_b, ctx_rows, ctx_seg):
    m, d = h.shape
    n = w.shape[1]
    tm, tn = 512, 1024
    assert tm % ctx_seg == 0 and ctx_seg % GRID_W == 0 and ctx_rows % tm == 0
    assert GRID_W >= 2 * HALO and CONV_K // 2 <= HALO
    kern = functools.partial(_xbc_kernel, ctx_rows=ctx_rows, ctx_seg=ctx_seg, lat_seg=GRID_W)
    stage_rows = tm + 2 * HALO
    return pl.pallas_call(
        kern,
        grid=(m // tm, n // tn),
        in_specs=[pl.BlockSpec((tm, d), lambda i, j: (i, 0)),
                  pl.BlockSpec((d, tn), lambda i, j: (0, j)),
                  pl.BlockSpec((CONV_K, tn), lambda i, j: (0, j)),
                  pl.BlockSpec((1, tn), lambda i, j: (0, j))],
        out_specs=pl.BlockSpec((tm, tn), lambda i, j: (i, j)),
        out_shape=jax.ShapeDtypeStruct((m, n), BF16),
        scratch_shapes=[pltpu.VMEM((stage_rows, tn), F32)],
        compiler_params=_cparams(("arbitrary", "arbitrary")),
        name="inproj_xbc",
    )(h, w, conv_w, conv_b.reshape(1, n))


def _dt_kernel(h_ref, w_ref, b_ref, o_ref):
    v = _dot(h_ref[...], w_ref[...]) + b_ref[...]
    o_ref[...] = jnp.maximum(v, 0.0) + jnp.log1p(jnp.exp(-jnp.abs(v)))


def _dt_call(h, w, b):
    m, d = h.shape
    n = w.shape[1]
    tm = 1024
    return pl.pallas_call(
        _dt_kernel,
        grid=(m // tm,),
        in_specs=[pl.BlockSpec((tm, d), lambda i: (i, 0)),
                  pl.BlockSpec((d, n), lambda i: (0, 0)),
                  pl.BlockSpec((1, n), lambda i: (0, 0))],
        out_specs=pl.BlockSpec((tm, n), lambda i: (i, 0)),
        out_shape=jax.ShapeDtypeStruct((m, n), F32),
        compiler_params=_cparams(("arbitrary",)),
        name="inproj_dt",
    )(h, w, b)


def _fnet_in_kernel(h_ref, w_ref, cs_ref, xc_ref, xs_ref):
    f = _dot(h_ref[...], w_ref[...]).astype(BF16)
    gw = cs_ref.shape[0]
    for g in range(FNET_GROUPS):
        sl = slice(g * gw, (g + 1) * gw)
        r = _dot(f[:, sl], cs_ref[...])
        xc_ref[:, sl] = r[:, :gw].astype(BF16)
        xs_ref[:, sl] = r[:, gw:].astype(BF16)


def _fnet_in_call(h, w, cs):
    m, d = h.shape
    n = w.shape[1]
    tm = 512
    spec_o = pl.BlockSpec((tm, n), lambda i: (i, 0))
    return pl.pallas_call(
        _fnet_in_kernel,
        grid=(m // tm,),
        in_specs=[pl.BlockSpec((tm, d), lambda i: (i, 0)),
                  pl.BlockSpec((d, n), lambda i: (0, 0)),
                  pl.BlockSpec(cs.shape, lambda i: (0, 0))],
        out_specs=[spec_o, spec_o],
        out_shape=[jax.ShapeDtypeStruct((m, n), BF16)] * 2,
        compiler_params=_cparams(("arbitrary",)),
        name="inproj_fnet",
    )(h, w, cs)


def _fnet_pos_kernel(xc_c_ref, xs_c_ref, xc_l_ref, xs_l_ref, cc_ref, sc_ref, cl_ref, sl_ref, o_ref,
                     *, n_ctx_tiles):
    i = pl.program_id(0)

    @pl.when(i < n_ctx_tiles)
    def _():
        o_ref[...] = (_dot(cc_ref[...], xc_c_ref[...]) - _dot(sc_ref[...], xs_c_ref[...])).astype(BF16)

    @pl.when(i >= n_ctx_tiles)
    def _():
        o_ref[...] = (_dot(cl_ref[...], xc_l_ref[...]) - _dot(sl_ref[...], xs_l_ref[...])).astype(BF16)


def _fnet_pos_call(xc, xs, cc, sc, cl, sl, ctx_rows):
    m, n = xc.shape
    tm = cc.shape[0]
    lat_len = cl.shape[0]
    n_ctx = ctx_rows // tm
    per_lat = lat_len // tm
    ctx_spec = pl.BlockSpec((tm, n), lambda i: (jnp.minimum(i, n_ctx - 1), 0))
    lat_spec = pl.BlockSpec(
        (lat_len, n), lambda i: (ctx_rows // lat_len + jnp.maximum(i - n_ctx, 0) // per_lat, 0))
    dft_c = pl.BlockSpec((tm, tm), lambda i: (0, 0))
    dft_l = pl.BlockSpec((tm, lat_len), lambda i: (jnp.maximum(i - n_ctx, 0) % per_lat, 0))
    kern = functools.partial(_fnet_pos_kernel, n_ctx_tiles=n_ctx)
    return pl.pallas_call(
        kern,
        grid=(m // tm,),
        in_specs=[ctx_spec, ctx_spec, lat_spec, lat_spec, dft_c, dft_c, dft_l, dft_l],
        out_specs=pl.BlockSpec((tm, n), lambda i: (i, 0)),
        out_shape=jax.ShapeDtypeStruct((m, n), BF16),
        compiler_params=_cparams(("arbitrary",)),
        name="fnet_pos",
    )(xc, xs, xc, xs, cc, sc, cl, sl)


def _sgu_kernel(h_ref, w_ref, g_ref, wsp_ref, bsp_ref, o_ref):
    tm, width = o_ref.shape
    hd = width // SGU_HEADS
    acc = _dot(h_ref[...], w_ref[...])
    k0 = math.sqrt(2.0 / math.pi)
    s = 0.5 * acc * (1.0 + jnp.tanh(k0 * (acc + 0.044715 * (acc * acc * acc))))
    u = s[:, :width]
    v = _rms(s[:, width:], g_ref[...]).astype(BF16)
    for k in range(tm // CHUNK):
        rows = slice(k * CHUNK, (k + 1) * CHUNK)
        for hh in range(SGU_HEADS):
            cols = slice(hh * hd, (hh + 1) * hd)
            sp = _dot(wsp_ref[hh], v[rows, cols]) + bsp_ref[:, hh:hh + 1]
            o_ref[rows, cols] = (u[rows, cols] * sp).astype(BF16)


def _sgu_call(h, w, g_sgu, w_sp, b_sp_t):
    m, d = h.shape
    n = w.shape[1]
    width = n // 2
    tm = 512
    return pl.pallas_call(
        _sgu_kernel,
        grid=(m // tm,),
        in_specs=[pl.BlockSpec((tm, d), lambda i: (i, 0)),
                  pl.BlockSpec((d, n), lambda i: (0, 0)),
                  pl.BlockSpec((1, width), lambda i: (0, 0)),
                  pl.BlockSpec(w_sp.shape, lambda i: (0, 0, 0)),
                  pl.BlockSpec(b_sp_t.shape, lambda i: (0, 0))],
        out_specs=pl.BlockSpec((tm, width), lambda i: (i, 0)),
        out_shape=jax.ShapeDtypeStruct((m, width), BF16),
        compiler_params=_cparams(("arbitrary",)),
        name="inproj_sgu",
    )(h, w, g_sgu.reshape(1, width), w_sp, b_sp_t)


def _ssd_kernel(tbl_ref, xs_ref, b_ref, c_ref, dt_ref, h0_ref, e3_ref, tri3_ref, alog_ref, *rest,
                reverse, final, n_ctx_seq, lane_off):
    if final:
        yprev_ref, zs_ref, dsk_ref, gssd_ref, y_ref, hfin_ref, st_ref, ybuf_ref = rest
    else:
        y_ref, hfin_ref, st_ref = rest
    s = pl.program_id(0)
    seq = tbl_ref[1, s]
    first = tbl_ref[2, s]
    last = tbl_ref[3, s]
    T = CHUNK
    hp = xs_ref.shape[1]
    gw = hp // SSD_GROUPS
    n_heads = hp // SSD_HEAD_DIM

    @pl.when(first == 1)
    def _():
        h0 = h0_ref[0, 0, 0].reshape(hp, SSD_STATE)
        h0 = jnp.where(seq < n_ctx_seq, 0.0, h0)
        st_ref[...] = h0.T

    row = lax.broadcasted_iota(jnp.int32, (T, T), 0)
    col = lax.broadcasted_iota(jnp.int32, (T, T), 1)
    tri = (row <= col) if reverse else (row >= col)

    dt = dt_ref[...]
    a = dt * (-jnp.exp(alog_ref[...]))
    acs = _dot(tri3_ref[...], jnp.concatenate(_split3(a), axis=0))
    end = acs[0:1, :] if reverse else acs[T - 1:T, :]
    lr_t = (acs - jnp.log(dt)).T
    w_t = (jnp.exp(end - acs) * dt).T
    ex = _dot(jnp.concatenate(_split3(jnp.exp(acs)), axis=1), e3_ref[...])
    dec = ex[0:1, :] if reverse else ex[T - 1:T, :]

    lane =lax.broadcasted_iota(jnp.int32, (1, LANES), 1)
    low_bf = jnp.where(lane < SSD_HEAD_DIM, 1.0, 0.0).astype(BF16)
    high_bf = jnp.where(lane < SSD_HEAD_DIM, 0.0, 1.0).astype(BF16)
    if final:
        dsum = dsk_ref[0:1, :] + dsk_ref[1:2, :]
        sq_acc = jnp.zeros((T, LANES), F32)

    heads_per_group = n_heads // SSD_GROUPS
    for g in range(SSD_GROUPS):
        bg = b_ref[:, g * SSD_STATE:(g + 1) * SSD_STATE]
        cg = c_ref[:, g * SSD_STATE:(g + 1) * SSD_STATE]
        gs = slice(g * gw, (g + 1) * gw)
        cb = lax.dot_general(cg, bg, (((1,), (1,)), ((), ())), preferred_element_type=F32)
        b_t = bg.astype(F32).T
        yoff = _dot(cg, st_ref[:, gs].astype(BF16))
        for q in range(heads_per_group // 2):
            ms, bs = [], []
            for r in (2 * q, 2 * q + 1):
                hl = lane_off + g * heads_per_group + r
                diff = acs[:, hl:hl + 1] - lr_t[hl:hl + 1, :]
                ms.append((cb * jnp.exp(jnp.where(tri, diff, -jnp.inf))).astype(BF16))
                bs.append((b_t * w_t[hl:hl + 1, :]).astype(BF16))
            c0 = g * gw + q * LANES
            cs_ = slice(c0, c0 + LANES)
            xpair = xs_ref[:, cs_]
            rhs = jnp.concatenate([xpair * low_bf, xpair * high_bf], axis=0)
            yd = _dot(jnp.concatenate(ms, axis=1), rhs)
            st_ref[:, cs_] = st_ref[:, cs_] * dec[:, cs_] + _dot(jnp.concatenate(bs, axis=1), rhs)
            yblk = yd + yoff[:, q * LANES:(q + 1) * LANES] * ex[:, cs_]
            if final:
                t = ((yblk + yprev_ref[:, cs_].astype(F32) + xpair.astype(F32) * dsum[:, cs_])
                     * zs_ref[:, cs_].astype(F32))
                sq_acc = sq_acc + t * t
                ybuf_ref[:, cs_] = t
            else:
                y_ref[:, cs_] = yblk.astype(BF16)

    if final:
        scale = lax.rsqrt(jnp.sum(sq_acc, axis=-1, keepdims=True) / hp + EPS)
        y_ref[...] = (ybuf_ref[...] * scale * gssd_ref[...]).astype(BF16)

    @pl.when(last == 1)
    def _():
        hfin_ref[0] = st_ref[...].T


def _ssd_tables(n_ctx_seq, ctx_chunks, n_lat_seq, lat_chunks, reverse):
    blk, seq, first, last = [], [], [], []
    base = 0
    for sidx, nch in [(i, ctx_chunks) for i in range(n_ctx_seq)] + \
                     [(n_ctx_seq + i, lat_chunks) for i in range(n_lat_seq)]:
        order = list(range(nch))[::-1] if reverse else list(range(nch))
        for k, cidx in enumerate(order):
            blk.append(base + cidx)
            seq.append(sidx)
            first.append(1 if k == 0 else 0)
            last.append(1 if k == nch - 1 else 0)
        base += nch
    return np.array([blk, seq, first, last], dtype=np.int32)


def _ssd_call(xbc, dt, state_ssd, layer, e3_mat, tri3, alog128, geom, *, reverse, final, extra=()):
    n_ctx_seq, ctx_chunks, n_lat_seq, lat_chunks = geom
    m = xbc.shape[0]
    hp = e3_mat.shape[1]
    n_groups_cols = SSD_GROUPS * SSD_STATE
    tbl = jnp.asarray(_ssd_tables(n_ctx_seq, ctx_chunks, n_lat_seq, lat_chunks, reverse))
    steps = tbl.shape[1]
    n_seq = n_ctx_seq + n_lat_seq
    direction = 1 if reverse else 0
    n_heads = hp // SSD_HEAD_DIM
    kern = functools.partial(_ssd_kernel, reverse=reverse, final=final, n_ctx_seq=n_ctx_seq,
                             lane_off=direction * n_heads)
    rowblk = lambda width, cblk: pl.BlockSpec((CHUNK, width), lambda s, t: (t[0, s], cblk))
    const2 = lambda shape: pl.BlockSpec(shape, lambda s, t: (0, 0))
    in_specs = [
        rowblk(hp, 0),
        rowblk(n_groups_cols, hp // n_groups_cols),
        rowblk(n_groups_cols, hp // n_groups_cols + 1),
        rowblk(LANES, 0),
        pl.BlockSpec((1, 1, 1) + state_ssd.shape[3:],
                     lambda s, t: (jnp.maximum(t[1, s] - n_ctx_seq, 0), layer, direction, 0, 0, 0)),
        const2(e3_mat.shape), const2(tri3.shape), const2(alog128.shape),
    ]
    args = [xbc, xbc, xbc, dt, state_ssd, e3_mat, tri3, alog128]
    scratch = [pltpu.VMEM((SSD_STATE, hp), F32)]
    if final:
        yprev, zs, dsk, gssd = extra
        in_specs += [rowblk(hp, 0), rowblk(hp, 0), const2(dsk.shape), const2(gssd.shape)]
        args += [yprev, zs, dsk, gssd]
        scratch.append(pltpu.VMEM((CHUNK, hp), F32))
    y_dtype = BF16
    grid_spec = pltpu.PrefetchScalarGridSpec(
        num_scalar_prefetch=1,
        grid=(steps,),
        in_specs=in_specs,
        out_specs=[rowblk(hp, 0),
                   pl.BlockSpec((1, hp, SSD_STATE), lambda s, t: (t[1, s], 0, 0))],
        scratch_shapes=scratch,
    )
    return pl.pallas_call(
        kern,
        grid_spec=grid_spec,
        out_shape=[jax.ShapeDtypeStruct((m, hp), y_dtype),
                   jax.ShapeDtypeStruct((n_seq, hp, SSD_STATE), F32)],
        compiler_params=_cparams(("arbitrary",)),
        name="ssd_bwd" if reverse else "ssd_fwd",
    )(tbl, *args)


def _mixout_kernel(ys_ref, yf_ref, yg_ref, x_ref, w_ref, mod_ref, ng_ref, xo_ref, ho_ref):
    ks = ys_ref.shape[1]
    kf = yf_ref.shape[1]
    mm = (_dot(ys_ref[...], w_ref[:ks, :]) + _dot(yf_ref[...], w_ref[ks:ks + kf, :])
          + _dot(yg_ref[...], w_ref[ks + kf:, :]))
    xn = x_ref[...] + mod_ref[0, 5:6, :] * _rms(mm, ng_ref[3:4, :])
    xo_ref[...] = xn
    hn = _rms(xn, ng_ref[4:5, :]) * (1.0 + mod_ref[0, 7:8, :]) + mod_ref[0, 6:7, :]
    ho_ref[...] = hn.astype(BF16)


def _mixout_call(yssd, yfnet, ysgu, x, w_out, mod, ng, rows_per_group):
    m, d = x.shape
    assert yssd.shape[1] + yfnet.shape[1] + ysgu.shape[1] == w_out.shape[0]
    tm = 512
    row = lambda width: pl.BlockSpec((tm, width), lambda i: (i, 0))
    return pl.pallas_call(
        _mixout_kernel,
        grid=(m // tm,),
        in_specs=[row(yssd.shape[1]), row(yfnet.shape[1]), row(ysgu.shape[1]), row(d),
                  pl.BlockSpec(w_out.shape, lambda i: (0, 0), pipeline_mode=pl.Buffered(1)),
                  pl.BlockSpec((1, N_MOD, d), lambda i: (i * tm // rows_per_group, 0, 0)),
                  pl.BlockSpec(ng.shape, lambda i: (0, 0))],
        out_specs=[row(d), row(d)],
        out_shape=[jax.ShapeDtypeStruct((m, d), F32), jax.ShapeDtypeStruct((m, d), BF16)],
        compiler_params=_cparams(("arbitrary",)),
        name="mixout",
    )(yssd, yfnet, ysgu, x, w_out, mod, ng)


def _dft_tables(n):
    scale = 1.0 / math.sqrt(n)
    r = int(round(math.sqrt(n)))
    if r * r != n or n < 1024:
        j = lax.broadcasted_iota(jnp.int32, (n, n), 0)
        k = lax.broadcasted_iota(jnp.int32, (n, n), 1)
        ang = ((j * k) % n).astype(F32) * (2.0 * math.pi / n)
        return (jnp.cos(ang) * scale).astype(BF16), (jnp.sin(ang) * scale).astype(BF16)
    jj = lax.broadcasted_iota(jnp.int32, (r, n), 0)
    kk = lax.broadcasted_iota(jnp.int32, (r, n), 1)
    a1 = ((jj * r * kk) % n).astype(F32) * (2.0 * math.pi / n)
    a2 = ((jj * kk) % n).astype(F32) * (2.0 * math.pi / n)
    c1, s1 = (jnp.cos(a1) * scale)[:, None, :], (jnp.sin(a1) * scale)[:, None, :]
    c2, s2 = jnp.cos(a2)[None, :, :], jnp.sin(a2)[None, :, :]
    cos = (c1 * c2 - s1 * s2).reshape(n, n)
    sin = (s1 * c2 + c1 * s2).reshape(n, n)
    return cos.astype(BF16), sin.astype(BF16)


def kernel(x_prompt, x_sample, state_ssd, c, c_ctx, w_mod, b_mod, norm_g, w_ffn1_up, w_ffn1_down,
           w_in, conv_w, conv_b, a_log, dt_bias, d_skip, g_ssd, g_sgu, w_sp, b_sp, w_out,
           w_ffn2_up, w_ffn2_down):
    batch, seq, d = x_prompt.shape
    dec_batch, dec_seq, _ = x_sample.shape
    depth = w_mod.shape[0]
    ctx_rows = batch * seq
    assert ctx_rows == dec_seq, "row tiles map to modulation groups in units of dec_seq rows"
    n_heads = a_log.shape[2]
    hp = n_heads * SSD_HEAD_DIM
    gn = SSD_GROUPS * SSD_STATE
    off_dt = hp + hp + 2 * gn
    off_fnet = off_dt + 2 * n_heads
    fnet_w = w_out.shape[1] // 4
    off_sgu = off_fnet + fnet_w

    x = jnp.concatenate([x_prompt.reshape(ctx_rows, d), x_sample.reshape(dec_batch * dec_seq, d)], axis=0)
    cvec = jnp.concatenate([c_ctx[None, :], c, jnp.zeros((8 - 1 - dec_batch, d), F32)], axis=0)
    mod_all = _mod_call(cvec, w_mod, b_mod).reshape(depth, 8, N_MOD, d)

    gw = fnet_w // FNET_GROUPS
    cc_g, sc_g = _dft_tables(gw)
    cs_chan = jnp.concatenate([cc_g, sc_g], axis=1)
    cc, sc = _dft_tables(seq)
    cl, sl = _dft_tables(dec_seq)
    e_np = np.zeros((2, LANES, hp), np.float32)
    for dr in range(2):
        for hh in range(n_heads):
            e_np[dr, dr * n_heads + hh, hh * SSD_HEAD_DIM:(hh + 1) * SSD_HEAD_DIM] = 1.0
    e3_mats = [jnp.asarray(np.tile(e_np[dr], (3, 1)), BF16) for dr in range(2)]
    tidx = np.arange(CHUNK)
    tri3s = [jnp.asarray(np.tile((tidx[:, None] >= tidx[None, :]).astype(np.float32), (1, 3)), BF16),
             jnp.asarray(np.tile((tidx[:, None] <= tidx[None, :]).astype(np.float32), (1, 3)), BF16)]
    geom = (batch, seq // CHUNK, dec_batch, dec_seq // CHUNK)

    h = _prenorm_call(x, mod_all[0], norm_g[0], dec_seq)
    states = []
    for l in range(depth):
        mod = mod_all[l]
        ng = norm_g[l]
        w_in_l = w_in[l]
        w_z = w_in_l[:, :hp].astype(BF16)
        w_xbc = w_in_l[:, hp:off_dt].astype(BF16)
        w_dt = jnp.pad(w_in_l[:, off_dt:off_fnet], ((0, 0), (0, LANES - 2 * n_heads))).astype(BF16)
        w_f = w_in_l[:, off_fnet:off_sgu].astype(BF16)
        w_s = w_in_l[:, off_sgu:].astype(BF16)
        dtb = jnp.pad(dt_bias[l].reshape(1, 2 * n_heads), ((0, 0), (0, LANES - 2 * n_heads)))
        alog128 = jnp.pad(a_log[l].reshape(1, 2 * n_heads), ((0, 0), (0, LANES - 2 * n_heads)))
        dskx = jnp.repeat(d_skip[l], SSD_HEAD_DIM, axis=1)

        x, h = _ffn_call(h, x, w_ffn1_up[l].astype(BF16), w_ffn1_down[l].astype(BF16), mod, ng, mod, ng,
                         dec_seq, gate=2, ng_out=1, ng_next=2, sh_next=3, sc_next=4, emit_next=True)

        zs = _z_call(h, w_z)
        xbc = _xbc_call(h, w_xbc, conv_w[l], conv_b[l], ctx_rows, seq)
        dt = _dt_call(h, w_dt, dtb)
        xc, xsn = _fnet_in_call(h, w_f, cs_chan)
        ysgu = _sgu_call(h, w_s, g_sgu[l], w_sp[l].astype(BF16), b_sp[l].T)

        y_f, hfin_f = _ssd_call(xbc, dt, state_ssd, l, e3_mats[0], tri3s[0], alog128, geom,
                                reverse=False, final=False)
        yssd, hfin_b = _ssd_call(xbc, dt, state_ssd, l, e3_mats[1], tri3s[1], alog128, geom,
                                 reverse=True, final=True,
                                 extra=(y_f, zs, dskx, g_ssd[l].reshape(1, hp)))
        states.append(jnp.stack([hfin_f[:batch], hfin_b[:batch]], axis=1))

        yfnet = _fnet_pos_call(xc, xsn, cc, sc, cl, sl, ctx_rows)
        x, h = _mixout_call(yssd, yfnet, ysgu, x, w_out[l].astype(BF16), mod, ng, dec_seq)

        last = l == depth - 1
        nxt = l if last else l + 1
        res = _ffn_call(h, x, w_ffn2_up[l].astype(BF16), w_ffn2_down[l].astype(BF16), mod, ng,
                        mod_all[nxt], norm_g[nxt], dec_seq, gate=8, ng_out=5, ng_next=0,
                        sh_next=0, sc_next=1, emit_next=not last, split_rows=ctx_rows)
        if not last:
            x, h = res

    y_prompt = res[0].reshape(batch, seq, d)
    y_sample = res[1].reshape(dec_batch, dec_seq, d)
    new_state = jnp.stack(states, axis=1).reshape(
        batch, depth, 2, n_heads, SSD_HEAD_DIM, SSD_STATE).astype(x_prompt.dtype)
    return (y_prompt, y_sample, new_state)
```

```python
import functools
import math

import numpy as np
import jax
import jax.numpy as jnp
from jax import lax
from jax.experimental import pallas as pl
from jax.experimental.pallas import tpu as pltpu

F32 = jnp.float32
BF16 = jnp.bfloat16
EPS = 1e-6

LANES = 128
VMEM_LIMIT = 62 * 1024 * 1024

SSD_HEAD_DIM = 64
SSD_GROUPS = 4
SSD_STATE = 128
CHUNK = 128
CONV_K = 5
GRID_W = 64
FNET_GROUPS = 4
SGU_HEADS = 8
N_MOD = 9


def _cparams(sem):
    return pltpu.CompilerParams(dimension_semantics=sem, vmem_limit_bytes=VMEM_LIMIT)


def _rms(x, g):
    ms = jnp.mean(x * x, axis=-1, keepdims=True)
    return x * lax.rsqrt(ms + EPS) * g


def _silu(x):
    return x * (1.0 / (1.0 + jnp.exp(-x)))


def _split3(v):
    hi = v.astype(BF16)
    r1 = v - hi.astype(F32)
    mid = r1.astype(BF16)
    lo = (r1 - mid.astype(F32)).astype(BF16)
    return hi, mid, lo


def _dot(a, b):
    return jnp.dot(a, b, preferred_element_type=F32)


def _mod_kernel(c_ref, w_ref, b_ref, o_ref):
    k = pl.program_id(1)
    s = _silu(c_ref[...]).astype(BF16)
    part = _dot(s, w_ref[0].astype(BF16))

    @pl.when(k == 0)
    def _():
        o_ref[0] = part + b_ref[0]

    @pl.when(k > 0)
    def _():
        o_ref[0] += part


def _mod_call(cvec, w_mod, b_mod):
    depth, d, n = w_mod.shape
    kc = 256
    return pl.pallas_call(
        _mod_kernel,
        grid=(depth, d // kc),
        in_specs=[
            pl.BlockSpec((8, kc), lambda l, k: (0, k)),
            pl.BlockSpec((1, kc, n), lambda l, k: (l, k, 0)),
            pl.BlockSpec((1, 1, n), lambda l, k: (l, 0, 0)),
        ],
        out_specs=pl.BlockSpec((1, 8, n), lambda l, k: (l, 0, 0)),
        out_shape=jax.ShapeDtypeStruct((depth, 8, n), F32),
        compiler_params=_cparams(("arbitrary", "arbitrary")),
        name="mod",
    )(cvec, w_mod, b_mod.reshape(depth, 1, n))


def _prenorm_kernel(x_ref, mod_ref, ng_ref, h_ref):
    h = _rms(x_ref[...], ng_ref[0:1, :]) * (1.0 + mod_ref[0, 1:2, :]) + mod_ref[0, 0:1, :]
    h_ref[...] = h.astype(BF16)


def _prenorm_call(x, mod, ng, rows_per_group):
    m, d = x.shape
    tm = 512
    return pl.pallas_call(
        _prenorm_kernel,
        grid=(m // tm,),
        in_specs=[
            pl.BlockSpec((tm, d), lambda i: (i, 0)),
            pl.BlockSpec((1, N_MOD, d), lambda i: (i * tm // rows_per_group, 0, 0)),
            pl.BlockSpec(ng.shape, lambda i: (0, 0)),
        ],
        out_specs=pl.BlockSpec((tm, d), lambda i: (i, 0)),
        out_shape=jax.ShapeDtypeStruct((m, d), BF16),
        compiler_params=_cparams(("arbitrary",)),
        name="prenorm",
    )(x, mod, ng)


def _ffn_kernel(h_ref, x_ref, wg_ref, wu_ref, wd_ref, mod_ref, ng_ref, modn_ref, ngn_ref,
                *rest, gate, ng_out, ng_next, sh_next, sc_next, emit_next, split_tile):
    if emit_next:
        xo_ref, ho_ref, acc_ref, act_ref = rest
        xo_refs = (xo_ref,)
    else:
        xo_a_ref, xo_b_ref, acc_ref, act_ref = rest
        xo_refs = (xo_a_ref, xo_b_ref)
    i = pl.program_id(0)
    j = pl.program_id(1)
    nf = pl.num_programs(1) - 2
    tm, d_model = acc_ref.shape
    th = tm // 2
    n_split = 4
    cw = d_model // n_split

    @pl.when(j == 0)
    def _():
        h = h_ref[...]
        act_ref[0] = (_silu(_dot(h, wg_ref[...])) * _dot(h, wu_ref[...])).astype(BF16)
        acc_ref[...] = jnp.zeros(acc_ref.shape, F32)

    @pl.when((j > 0) & (j < nf))
    def _():
        h = h_ref[...]
        g = _dot(h, wg_ref[...])
        u = _dot(h, wu_ref[...])
        a_prev = act_ref[(j + 1) % 2]
        for cblk in range(n_split):
            cols = slice(cblk * cw, (cblk + 1) * cw)
            acc_ref[:, cols] += _dot(a_prev, wd_ref[:, cols])
        act_ref[j % 2] = (_silu(g) * u).astype(BF16)

    def finish(y):
        xn = x_ref[...] + 0.5 * mod_ref[0, gate:gate + 1, :] * _rms(y, ng_ref[ng_out:ng_out + 1, :])
        if emit_next:
            xo_ref[...] = xn
            hn = (_rms(xn, ngn_ref[ng_next:ng_next + 1, :])
                  * (1.0 + modn_ref[0, sc_next:sc_next + 1, :]) + modn_ref[0, sh_next:sh_next + 1, :])
            ho_ref[...] = hn.astype(BF16)
        else:
            @pl.when(i < split_tile)
            def _():
                xo_refs[0][...] = xn

            @pl.when(i >= split_tile)
            def _():
                xo_refs[1][...] = xn

    @pl.when(j == nf)
    def _():
        a_prev = act_ref[(j + 1) % 2]
        acc_ref[th:, :] += _dot(a_prev[th:, :], wd_ref[...])
        finish(acc_ref[:th, :] + _dot(a_prev[:th, :], wd_ref[...]))

    @pl.when(j == nf + 1)
    def _():
        finish(acc_ref[th:, :])


def _ffn_call(h, x, w_up, w_down, mod, ng, mod_next, ng_next_arr, rows_per_group, *, gate, ng_out,
              ng_next, sh_next, sc_next, emit_next, split_rows=0):
    m, d = x.shape
    f = w_down.shape[0]
    tm, fc = 1024, 512
    th = tm // 2
    nf = f // fc
    kern = functools.partial(_ffn_kernel, gate=gate, ng_out=ng_out, ng_next=ng_next,
                             sh_next=sh_next, sc_next=sc_next, emit_next=emit_next,
                             split_tile=split_rows // tm)
    half = lambda i, j: 2 * i + jnp.where(j > nf, 1, 0)
    row_h = pl.BlockSpec((tm, d), lambda i, j: (i, 0))
    row_x = pl.BlockSpec((th, d), lambda i, j: (half(i, j), 0))
    modspec = pl.BlockSpec((1, N_MOD, d), lambda i, j: (i * tm // rows_per_group, 0, 0))
    ngspec = pl.BlockSpec(ng.shape, lambda i, j: (0, 0))
    if emit_next:
        out_specs = [row_x, row_x]
        out_shape = [jax.ShapeDtypeStruct((m, d), F32), jax.ShapeDtypeStruct((m, d), BF16)]
    else:
        assert split_rows % tm == 0
        na = split_rows // th
        out_specs = [pl.BlockSpec((th, d), lambda i, j: (jnp.minimum(half(i, j), na - 1), 0)),
                     pl.BlockSpec((th, d), lambda i, j: (jnp.maximum(half(i, j) - na, 0), 0))]
        out_shape = [jax.ShapeDtypeStruct((split_rows, d), F32),
                     jax.ShapeDtypeStruct((m - split_rows, d), F32)]
    return pl.pallas_call(
        kern,
        grid=(m // tm, nf + 2),
        in_specs=[
            row_h, row_x,
            pl.BlockSpec((d, fc), lambda i, j: (0, jnp.minimum(j, nf - 1))),
            pl.BlockSpec((d, fc), lambda i, j: (0, jnp.minimum(j, nf - 1) + nf)),
            pl.BlockSpec((fc, d), lambda i, j: (jnp.clip(j - 1, 0, nf - 1), 0)),
            modspec, ngspec, modspec, ngspec,
        ],
        out_specs=out_specs,
        out_shape=out_shape,
        scratch_shapes=[pltpu.VMEM((tm, d), F32), pltpu.VMEM((2, tm, fc), BF16)],
        compiler_params=_cparams(("arbitrary", "arbitrary")),
        name="ffn",
    )(h, x, w_up, w_up, w_down, mod, ng, mod_next, ng_next_arr)


def _z_kernel(h_ref, w_ref, o_ref):
    o_ref[...] = _silu(_dot(h_ref[...], w_ref[...])).astype(BF16)


def _z_call(h, w):
    m, d = h.shape
    n = w.shape[1]
    tm, tn = 1024, 1024
    return pl.pallas_call(
        _z_kernel,
        grid=(m // tm, n // tn),
        in_specs=[pl.BlockSpec((tm, d), lambda i, j: (i, 0)),
                  pl.BlockSpec((d, tn), lambda i, j: (0, j))],
        out_specs=pl.BlockSpec((tm, tn), lambda i, j: (i, j)),
        out_shape=jax.ShapeDtypeStruct((m, n), BF16),
        compiler_params=_cparams(("arbitrary", "arbitrary")),
        name="inproj_z",
    )(h, w)


HALO = 8


def _xbc_kernel(h_ref, w_ref, cw_ref, cb_ref, o_ref, s_ref, t_ref, *, ctx_rows, ctx_seg, lat_seg):
    tm, tn = o_ref.shape
    i = pl.program_id(0)
    half = CONV_K // 2
    cwid = 2 * LANES
    per = ctx_seg // lat_seg
    nv = lat_seg // 8
    is_lat = i * tm >= ctx_rows
    sub = lax.broadcasted_iota(jnp.int32, (8, LANES), 0)

    @pl.when((i == 0) & (pl.program_id(1) == 0))
    def _():
        for c in range(tn // LANES):
            s_ref[c, 0:HALO, :] = jnp.zeros((HALO, LANES), F32)
            s_ref[c, HALO + tm:2 * HALO + tm, :] = jnp.zeros((HALO, LANES), F32)

    def pad_select(v, crosses, static_boundary):
        if static_boundary:
            return jnp.where(crosses, 0.0, v)
        return jnp.where(jnp.logical_and(crosses, is_lat), 0.0, v)

    for nb in range(tn // cwid):
        acc = _dot(h_ref[...], w_ref[:, nb * cwid:(nb + 1) * cwid])
        for c2 in range(cwid // LANES):
            s_ref[nb * (cwid // LANES) + c2, HALO:HALO + tm, :] = acc[:, c2 * LANES:(c2 + 1) * LANES]
        for c2 in range(cwid // LANES):
            c = nb * (cwid // LANES) + c2
            lanes = slice(c * LANES, (c + 1) * LANES)
            wts = [cw_ref[k:k + 1, lanes] for k in range(CONV_K)]
            bias = cb_ref[:, lanes]
            for g in range(tm // lat_seg):
                r0 = HALO + g * lat_seg
                vec = {}
                for e in range(-half, nv + half):
                    v = s_ref[c, pl.ds(r0 + e, 8, stride=8), :]
                    if e < 0:
                        v = pad_select(v, sub == 0, g % per == 0)
                    if e >= nv:
                        v = pad_select(v, sub == 7, g % per == per - 1)
                    vec[e] = v
                for a in range(nv):
                    out = bias
                    for dlt in range(-half, half + 1):
                        out = out + vec[a + dlt] * wts[half + dlt]
                    t_ref[c, pl.ds(g * lat_seg + a, 8, stride=8), :] = _silu(out)
            o_ref[:, lanes] = t_ref[c].astype(BF16)


def _xbc_call(h, w, conv_w, conv_b, ctx_rows, ctx_seg):
    m, d = h.shape
    n = w.shape[1]
    tm, tn = 512, 1024
    assert tm % ctx_seg == 0 and ctx_seg % GRID_W == 0 and ctx_rows % tm == 0
    assert GRID_W == 64 and CONV_K // 2 <= HALO
    kern = functools.partial(_xbc_kernel, ctx_rows=ctx_rows, ctx_seg=ctx_seg, lat_seg=GRID_W)
    stage_rows = tm + 2 * HALO
    return pl.pallas_call(
        kern,
        grid=(m // tm, n // tn),
        in_specs=[pl.BlockSpec((tm, d), lambda i, j: (i, 0)),
                  pl.BlockSpec((d, tn), lambda i, j: (0, j)),
                  pl.BlockSpec((CONV_K, tn), lambda i, j: (0, j)),
                  pl.BlockSpec((1, tn), lambda i, j: (0, j))],
        out_specs=pl.BlockSpec((tm, tn), lambda i, j: (i, j)),
        out_shape=jax.ShapeDtypeStruct((m, n), BF16),
        scratch_shapes=[pltpu.VMEM((tn // LANES, stage_rows, LANES), F32),
                        pltpu.VMEM((tn // LANES, tm, LANES), F32)],
        compiler_params=_cparams(("arbitrary", "arbitrary")),
        name="inproj_xbc",
    )(h, w, conv_w, conv_b.reshape(1, n))


def _dt_kernel(h_ref, w_ref, b_ref, o_ref):
    v = _dot(h_ref[...], w_ref[...]) + b_ref[...]
    o_ref[...] = jnp.maximum(v, 0.0) + jnp.log1p(jnp.exp(-jnp.abs(v)))


def _dt_call(h, w, b):
    m, d = h.shape
    n = w.shape[1]
    tm = 1024
    return pl.pallas_call(
        _dt_kernel,
        grid=(m // tm,),
        in_specs=[pl.BlockSpec((tm, d), lambda i: (i, 0)),
                  pl.BlockSpec((d, n), lambda i: (0, 0)),
                  pl.BlockSpec((1, n), lambda i: (0, 0))],
        out_specs=pl.BlockSpec((tm, n), lambda i: (i, 0)),
        out_shape=jax.ShapeDtypeStruct((m, n), F32),
        compiler_params=_cparams(("arbitrary",)),
        name="inproj_dt",
    )(h, w, b)


def _fnet_in_kernel(h_ref, w_ref, cs_ref, xc_ref, xs_ref):
    f = _dot(h_ref[...], w_ref[...]).astype(BF16)
    gw = cs_ref.shape[0]
    for g in range(FNET_GROUPS):
        sl = slice(g * gw, (g + 1) * gw)
        r = _dot(f[:, sl], cs_ref[...])
        xc_ref[:, sl] = r[:, :gw].astype(BF16)
        xs_ref[:, sl] = r[:, gw:].astype(BF16)


def _fnet_in_call(h, w, cs):
    m, d = h.shape
    n = w.shape[1]
    tm = 512
    spec_o = pl.BlockSpec((tm, n), lambda i: (i, 0))
    return pl.pallas_call(
        _fnet_in_kernel,
        grid=(m // tm,),
        in_specs=[pl.BlockSpec((tm, d), lambda i: (i, 0)),
                  pl.BlockSpec((d, n), lambda i: (0, 0)),
                  pl.BlockSpec(cs.shape, lambda i: (0, 0))],
        out_specs=[spec_o, spec_o],
        out_shape=[jax.ShapeDtypeStruct((m, n), BF16)] * 2,
        compiler_params=_cparams(("arbitrary",)),
        name="inproj_fnet",
    )(h, w, cs)


def _fnet_pos_kernel(xc_c_ref, xs_c_ref, xc_l_ref, xs_l_ref, cc_ref, sc_ref, cl_ref, sl_ref, o_ref,
                     *, n_ctx_tiles):
    i = pl.program_id(0)

    @pl.when(i < n_ctx_tiles)
    def _():
        o_ref[...] = (_dot(cc_ref[...], xc_c_ref[...]) - _dot(sc_ref[...], xs_c_ref[...])).astype(BF16)

    @pl.when(i >= n_ctx_tiles)
    def _():
        o_ref[...] = (_dot(cl_ref[...], xc_l_ref[...]) - _dot(sl_ref[...], xs_l_ref[...])).astype(BF16)


def _fnet_pos_call(xc, xs, cc, sc, cl, sl, ctx_rows):
    m, n = xc.shape
    tm = cc.shape[0]
    lat_len = cl.shape[0]
    n_ctx = ctx_rows // tm
    per_lat = lat_len // tm
    ctx_spec = pl.BlockSpec((tm, n), lambda i: (jnp.minimum(i, n_ctx - 1), 0))
    lat_spec = pl.BlockSpec(
        (lat_len, n), lambda i: (ctx_rows // lat_len + jnp.maximum(i - n_ctx, 0) // per_lat, 0))
    dft_c = pl.BlockSpec((tm, tm), lambda i: (0, 0))
    dft_l = pl.BlockSpec((tm, lat_len), lambda i: (jnp.maximum(i - n_ctx, 0) % per_lat, 0))
    kern = functools.partial(_fnet_pos_kernel, n_ctx_tiles=n_ctx)
    return pl.pallas_call(
        kern,
        grid=(m // tm,),
        in_specs=[ctx_spec, ctx_spec, lat_spec, lat_spec, dft_c, dft_c, dft_l, dft_l],
        out_specs=pl.BlockSpec((tm, n), lambda i: (i, 0)),
        out_shape=jax.ShapeDtypeStruct((m, n), BF16),
        compiler_params=_cparams(("arbitrary",)),
        name="fnet_pos",
    )(xc, xs, xc, xs, cc, sc, cl, sl)


def _sgu_kernel(h_ref, w_ref, g_ref, wsp_ref, bsp_ref, o_ref):
    tm, width = o_ref.shape
    hd = width // SGU_HEADS
    acc = _dot(h_ref[...], w_ref[...])
    k0 = math.sqrt(2.0 / math.pi)
    s = 0.5 * acc * (1.0 + jnp.tanh(k0 * (acc + 0.044715 * (acc * acc * acc))))
    u = s[:, :width]
    v = _rms(s[:, width:], g_ref[...]).astype(BF16)
    for k in range(tm // CHUNK):
        rows = slice(k * CHUNK, (k + 1) * CHUNK)
        for hh in range(SGU_HEADS):
            cols = slice(hh * hd, (hh + 1) * hd)
            sp = _dot(wsp_ref[hh], v[rows, cols]) + bsp_ref[:, hh:hh + 1]
            o_ref[rows, cols] = (u[rows, cols] * sp).astype(BF16)


def _sgu_call(h, w, g_sgu, w_sp, b_sp_t):
    m, d = h.shape
    n = w.shape[1]
    width = n // 2
    tm = 512
    return pl.pallas_call(
        _sgu_kernel,
        grid=(m // tm,),
        in_specs=[pl.BlockSpec((tm, d), lambda i: (i, 0)),
                  pl.BlockSpec((d, n), lambda i: (0, 0)),
                  pl.BlockSpec((1, width), lambda i: (0, 0)),
                  pl.BlockSpec(w_sp.shape, lambda i: (0, 0, 0)),
                  pl.BlockSpec(b_sp_t.shape, lambda i: (0, 0))],
        out_specs=pl.BlockSpec((tm, width), lambda i: (i, 0)),
        out_shape=jax.ShapeDtypeStruct((m, width), BF16),
        compiler_params=_cparams(("arbitrary",)),
        name="inproj_sgu",
    )(h, w, g_sgu.reshape(1, width), w_sp, b_sp_t)


def _ssd_kernel(tbl_ref, xs_ref, b_ref, c_ref, dt_ref, h0_ref, e3_ref, tri3_ref, alog_ref, *rest,
                reverse, final, n_ctx_seq, lane_off):
    if final:
        yprev_ref, zs_ref, dsk_ref, gssd_ref, y_ref, hfin_ref, st_ref, ybuf_ref = rest
    else:
        y_ref, hfin_ref, st_ref = rest
    s = pl.program_id(0)
    seq = tbl_ref[1, s]
    first = tbl_ref[2, s]
    last = tbl_ref[3, s]
    T = CHUNK
    hp = xs_ref.shape[1]
    gw = hp // SSD_GROUPS
    n_heads = hp // SSD_HEAD_DIM

    @pl.when(first == 1)
    def _():
        h0 = h0_ref[0, 0, 0].reshape(hp, SSD_STATE)
        h0 = jnp.where(seq < n_ctx_seq, 0.0, h0)
        st_ref[...] = h0.T

    row = lax.broadcasted_iota(jnp.int32, (T, T), 0)
    col = lax.broadcasted_iota(jnp.int32, (T, T), 1)
    tri = (row <= col) if reverse else (row >= col)

    dt = dt_ref[...]
    a = dt * (-jnp.exp(alog_ref[...]))
    acs = _dot(tri3_ref[...], jnp.concatenate(_split3(a), axis=0))
    end = acs[0:1, :] if reverse else acs[T - 1:T, :]
    lr_t = (acs - jnp.log(dt)).T
    w_t = (jnp.exp(end - acs) * dt).T
    ex = _dot(jnp.concatenate(_split3(jnp.exp(acs)), axis=1), e3_ref[...])
    dec = ex[0:1, :] if reverse else ex[T - 1:T, :]

    lane =lax.broadcasted_iota(jnp.int32, (1, LANES), 1)
    low_bf = jnp.where(lane < SSD_HEAD_DIM, 1.0, 0.0).astype(BF16)
    high_bf = jnp.where(lane < SSD_HEAD_DIM, 0.0, 1.0).astype(BF16)
    if final:
        dsum = dsk_ref[0:1, :] + dsk_ref[1:2, :]
        sq_acc = jnp.zeros((T, LANES), F32)

    heads_per_group = n_heads // SSD_GROUPS
    for g in range(SSD_GROUPS):
        bg = b_ref[:, g * SSD_STATE:(g + 1) * SSD_STATE]
        cg = c_ref[:, g * SSD_STATE:(g + 1) * SSD_STATE]
        gs = slice(g * gw, (g + 1) * gw)
        cb = lax.dot_general(cg, bg, (((1,), (1,)), ((), ())), preferred_element_type=F32)
        b_t = bg.astype(F32).T
        yoff = _dot(cg, st_ref[:, gs].astype(BF16))
        for q in range(heads_per_group // 2):
            ms, bs = [], []
            for r in (2 * q, 2 * q + 1):
                hl = lane_off + g * heads_per_group + r
                diff = acs[:, hl:hl + 1] - lr_t[hl:hl + 1, :]
                ms.append((cb * jnp.exp(jnp.where(tri, diff, -jnp.inf))).astype(BF16))
                bs.append((b_t * w_t[hl:hl + 1, :]).astype(BF16))
            c0 = g * gw + q * LANES
            cs_ = slice(c0, c0 + LANES)
            xpair = xs_ref[:, cs_]
            rhs = jnp.concatenate([xpair * low_bf, xpair * high_bf], axis=0)
            yd = _dot(jnp.concatenate(ms, axis=1), rhs)
            st_ref[:, cs_] = st_ref[:, cs_] * dec[:, cs_] + _dot(jnp.concatenate(bs, axis=1), rhs)
            yblk = yd + yoff[:, q * LANES:(q + 1) * LANES] * ex[:, cs_]
            if final:
                t = ((yblk + yprev_ref[:, cs_].astype(F32) + xpair.astype(F32) * dsum[:, cs_])
                     * zs_ref[:, cs_].astype(F32))
                sq_acc = sq_acc + t * t
                ybuf_ref[:, cs_] = t
            else:
                y_ref[:, cs_] = yblk.astype(BF16)

    if final:
        scale = lax.rsqrt(jnp.sum(sq_acc, axis=-1, keepdims=True) / hp + EPS)
        y_ref[...] = (ybuf_ref[...] * scale * gssd_ref[...]).astype(BF16)

    @pl.when(last == 1)
    def _():
        hfin_ref[0] = st_ref[...].T


def _ssd_tables(n_ctx_seq, ctx_chunks, n_lat_seq, lat_chunks, reverse):
    blk, seq, first, last = [], [], [], []
    base = 0
    for sidx, nch in [(i, ctx_chunks) for i in range(n_ctx_seq)] + \
                     [(n_ctx_seq + i, lat_chunks) for i in range(n_lat_seq)]:
        order = list(range(nch))[::-1] if reverse else list(range(nch))
        for k, cidx in enumerate(order):
            blk.append(base + cidx)
            seq.append(sidx)
            first.append(1 if k == 0 else 0)
            last.append(1 if k == nch - 1 else 0)
        base += nch
    return np.array([blk, seq, first, last], dtype=np.int32)


def _ssd_call(xbc, dt, state_ssd, layer, e3_mat, tri3, alog128, geom, *, reverse, final, extra=()):
    n_ctx_seq, ctx_chunks, n_lat_seq, lat_chunks = geom
    m = xbc.shape[0]
    hp = e3_mat.shape[1]
    n_groups_cols = SSD_GROUPS * SSD_STATE
    tbl = jnp.asarray(_ssd_tables(n_ctx_seq, ctx_chunks, n_lat_seq, lat_chunks, reverse))
    steps = tbl.shape[1]
    n_seq = n_ctx_seq + n_lat_seq
    direction = 1 if reverse else 0
    n_heads = hp // SSD_HEAD_DIM
    kern = functools.partial(_ssd_kernel, reverse=reverse, final=final, n_ctx_seq=n_ctx_seq,
                             lane_off=direction * n_heads)
    rowblk = lambda width, cblk: pl.BlockSpec((CHUNK, width), lambda s, t: (t[0, s], cblk))
    const2 = lambda shape: pl.BlockSpec(shape, lambda s, t: (0, 0))
    in_specs = [
        rowblk(hp, 0),
        rowblk(n_groups_cols, hp // n_groups_cols),
        rowblk(n_groups_cols, hp // n_groups_cols + 1),
        rowblk(LANES, 0),
        pl.BlockSpec((1, 1, 1) + state_ssd.shape[3:],
                     lambda s, t: (jnp.maximum(t[1, s] - n_ctx_seq, 0), layer, direction, 0, 0, 0)),
        const2(e3_mat.shape), const2(tri3.shape), const2(alog128.shape),
    ]
    args = [xbc, xbc, xbc, dt, state_ssd, e3_mat, tri3, alog128]
    scratch = [pltpu.VMEM((SSD_STATE, hp), F32)]
    if final:
        yprev, zs, dsk, gssd = extra
        in_specs += [rowblk(hp, 0), rowblk(hp, 0), const2(dsk.shape), const2(gssd.shape)]
        args += [yprev, zs, dsk, gssd]
        scratch.append(pltpu.VMEM((CHUNK, hp), F32))
    y_dtype = BF16
    grid_spec = pltpu.PrefetchScalarGridSpec(
        num_scalar_prefetch=1,
        grid=(steps,),
        in_specs=in_specs,
        out_specs=[rowblk(hp, 0),
                   pl.BlockSpec((1, hp, SSD_STATE), lambda s, t: (t[1, s], 0, 0))],
        scratch_shapes=scratch,
    )
    return pl.pallas_call(
        kern,
        grid_spec=grid_spec,
        out_shape=[jax.ShapeDtypeStruct((m, hp), y_dtype),
                   jax.ShapeDtypeStruct((n_seq, hp, SSD_STATE), F32)],
        compiler_params=_cparams(("arbitrary",)),
        name="ssd_bwd" if reverse else "ssd_fwd",
    )(tbl, *args)


def _mixout_kernel(ys_ref, yf_ref, yg_ref, x_ref, w_ref, mod_ref, ng_ref, xo_ref, ho_ref):
    ks = ys_ref.shape[1]
    kf = yf_ref.shape[1]
    tm = xo_ref.shape[0]
    n_sub = 2
    for r in range(n_sub):
        rows = slice(r * (tm // n_sub), (r + 1) * (tm // n_sub))
        mm = (_dot(ys_ref[rows, :], w_ref[:ks, :]) + _dot(yf_ref[rows, :], w_ref[ks:ks + kf, :])
              + _dot(yg_ref[rows, :], w_ref[ks + kf:, :]))
        xn = x_ref[rows, :] + mod_ref[0, 5:6, :] * _rms(mm, ng_ref[3:4, :])
        xo_ref[rows, :] = xn
        hn = _rms(xn, ng_ref[4:5, :]) * (1.0 + mod_ref[0, 7:8, :]) + mod_ref[0, 6:7, :]
        ho_ref[rows, :] = hn.astype(BF16)


def _mixout_call(yssd, yfnet, ysgu, x, w_out, mod, ng, rows_per_group):
    m, d = x.shape
    assert yssd.shape[1] + yfnet.shape[1] + ysgu.shape[1] == w_out.shape[0]
    tm = 512
    row = lambda width: pl.BlockSpec((tm, width), lambda i: (i, 0))
    return pl.pallas_call(
        _mixout_kernel,
        grid=(m // tm,),
        in_specs=[row(yssd.shape[1]), row(yfnet.shape[1]), row(ysgu.shape[1]), row(d),
                  pl.BlockSpec(w_out.shape, lambda i: (0, 0), pipeline_mode=pl.Buffered(1)),
                  pl.BlockSpec((1, N_MOD, d), lambda i: (i * tm // rows_per_group, 0, 0)),
                  pl.BlockSpec(ng.shape, lambda i: (0, 0))],
        out_specs=[row(d), row(d)],
        out_shape=[jax.ShapeDtypeStruct((m, d), F32), jax.ShapeDtypeStruct((m, d), BF16)],
        compiler_params=_cparams(("arbitrary",)),
        name="mixout",
    )(yssd, yfnet, ysgu, x, w_out, mod, ng)


def _dft_tables(n):
    scale = 1.0 / math.sqrt(n)
    r = int(round(math.sqrt(n)))
    if r * r != n or n < 1024:
        j = lax.broadcasted_iota(jnp.int32, (n, n), 0)
        k = lax.broadcasted_iota(jnp.int32, (n, n), 1)
        ang = ((j * k) % n).astype(F32) * (2.0 * math.pi / n)
        return (jnp.cos(ang) * scale).astype(BF16), (jnp.sin(ang) * scale).astype(BF16)
    jj = lax.broadcasted_iota(jnp.int32, (r, n), 0)
    kk = lax.broadcasted_iota(jnp.int32, (r, n), 1)
    a1 = ((jj * r * kk) % n).astype(F32) * (2.0 * math.pi / n)
    a2 = ((jj * kk) % n).astype(F32) * (2.0 * math.pi / n)
    c1, s1 = (jnp.cos(a1) * scale)[:, None, :], (jnp.sin(a1) * scale)[:, None, :]
    c2, s2 = jnp.cos(a2)[None, :, :], jnp.sin(a2)[None, :, :]
    cos = (c1 * c2 - s1 * s2).reshape(n, n)
    sin = (s1 * c2 + c1 * s2).reshape(n, n)
    return cos.astype(BF16), sin.astype(BF16)


def kernel(x_prompt, x_sample, state_ssd, c, c_ctx, w_mod, b_mod, norm_g, w_ffn1_up, w_ffn1_down,
           w_in, conv_w, conv_b, a_log, dt_bias, d_skip, g_ssd, g_sgu, w_sp, b_sp, w_out,
           w_ffn2_up, w_ffn2_down):
    batch, seq, d = x_prompt.shape
    dec_batch, dec_seq, _ = x_sample.shape
    depth = w_mod.shape[0]
    ctx_rows = batch * seq
    assert ctx_rows == dec_seq, "row tiles map to modulation groups in units of dec_seq rows"
    n_heads = a_log.shape[2]
    hp = n_heads * SSD_HEAD_DIM
    gn = SSD_GROUPS * SSD_STATE
    off_dt = hp + hp + 2 * gn
    off_fnet = off_dt + 2 * n_heads
    fnet_w = w_out.shape[1] // 4
    off_sgu = off_fnet + fnet_w

    x = jnp.concatenate([x_prompt.reshape(ctx_rows, d), x_sample.reshape(dec_batch * dec_seq, d)], axis=0)
    cvec = jnp.concatenate([c_ctx[None, :], c, jnp.zeros((8 - 1 - dec_batch, d), F32)], axis=0)
    mod_all = _mod_call(cvec, w_mod, b_mod).reshape(depth, 8, N_MOD, d)

    gw = fnet_w // FNET_GROUPS
    cc_g, sc_g = _dft_tables(gw)
    cs_chan = jnp.concatenate([cc_g, sc_g], axis=1)
    cc, sc = _dft_tables(seq)
    cl, sl = _dft_tables(dec_seq)
    e_np = np.zeros((2, LANES, hp), np.float32)
    for dr in range(2):
        for hh in range(n_heads):
            e_np[dr, dr * n_heads + hh, hh * SSD_HEAD_DIM:(hh + 1) * SSD_HEAD_DIM] = 1.0
    e3_mats = [jnp.asarray(np.tile(e_np[dr], (3, 1)), BF16) for dr in range(2)]
    tidx = np.arange(CHUNK)
    tri3s = [jnp.asarray(np.tile((tidx[:, None] >= tidx[None, :]).astype(np.float32), (1, 3)), BF16),
             jnp.asarray(np.tile((tidx[:, None] <= tidx[None, :]).astype(np.float32), (1, 3)), BF16)]
    geom = (batch, seq // CHUNK, dec_batch, dec_seq // CHUNK)

    h = _prenorm_call(x, mod_all[0], norm_g[0], dec_seq)
    states = []
    for l in range(depth):
        mod = mod_all[l]
        ng = norm_g[l]
        w_in_l = w_in[l]
        w_z = w_in_l[:, :hp].astype(BF16)
        w_xbc = w_in_l[:, hp:off_dt].astype(BF16)
        w_dt = jnp.pad(w_in_l[:, off_dt:off_fnet], ((0, 0), (0, LANES - 2 * n_heads))).astype(BF16)
        w_f = w_in_l[:, off_fnet:off_sgu].astype(BF16)
        w_s = w_in_l[:, off_sgu:].astype(BF16)
        dtb = jnp.pad(dt_bias[l].reshape(1, 2 * n_heads), ((0, 0), (0, LANES - 2 * n_heads)))
        alog128 = jnp.pad(a_log[l].reshape(1, 2 * n_heads), ((0, 0), (0, LANES - 2 * n_heads)))
        dskx = jnp.repeat(d_skip[l], SSD_HEAD_DIM, axis=1)

        x, h = _ffn_call(h, x, w_ffn1_up[l].astype(BF16), w_ffn1_down[l].astype(BF16), mod, ng, mod, ng,
                         dec_seq, gate=2, ng_out=1, ng_next=2, sh_next=3, sc_next=4, emit_next=True)

        zs = _z_call(h, w_z)
        xbc = _xbc_call(h, w_xbc, conv_w[l], conv_b[l], ctx_rows, seq)
        dt = _dt_call(h, w_dt, dtb)
        xc, xsn = _fnet_in_call(h, w_f, cs_chan)
        ysgu = _sgu_call(h, w_s, g_sgu[l], w_sp[l].astype(BF16), b_sp[l].T)

        y_f, hfin_f = _ssd_call(xbc, dt, state_ssd, l, e3_mats[0], tri3s[0], alog128, geom,
                                reverse=False, final=False)
        yssd, hfin_b = _ssd_call(xbc, dt, state_ssd, l, e3_mats[1], tri3s[1], alog128, geom,
                                 reverse=True, final=True,
                                 extra=(y_f, zs, dskx, g_ssd[l].reshape(1, hp)))
        states.append(jnp.stack([hfin_f[:batch], hfin_b[:batch]], axis=1))

        yfnet = _fnet_pos_call(xc, xsn, cc, sc, cl, sl, ctx_rows)
        x, h = _mixout_call(yssd, yfnet, ysgu, x, w_out[l].astype(BF16), mod, ng, dec_seq)

        last = l == depth - 1
        nxt = l if last else l + 1
        res = _ffn_call(h, x, w_ffn2_up[l].astype(BF16), w_ffn2_down[l].astype(BF16), mod, ng,
                        mod_all[nxt], norm_g[nxt], dec_seq, gate=8, ng_out=5, ng_next=0,
                        sh_next=0, sc_next=1, emit_next=not last, split_rows=ctx_rows)
        if not last:
            x, h = res

    y_prompt = res[0].reshape(batch, seq, d)
    y_sample = res[1].reshape(dec_batch, dec_seq, d)
    new_state = jnp.stack(states, axis=1).reshape(
        batch, depth, 2, n_heads, SSD_HEAD_DIM, SSD_STATE).astype(x_prompt.dtype)
    return (y_prompt, y_sample, new_state)
```

```python
import functools
import math

import numpy as np
import jax
import jax.numpy as jnp
from jax import lax
from jax.experimental import pallas as pl
from jax.experimental.pallas import tpu as pltpu

F32 = jnp.float32
BF16 = jnp.bfloat16
EPS = 1e-6

LANES = 128
VMEM_LIMIT = 62 * 1024 * 1024

SSD_HEAD_DIM = 64
SSD_GROUPS = 4
SSD_STATE = 128
CHUNK = 128
CONV_K = 5
GRID_W = 64
FNET_GROUPS = 4
SGU_HEADS = 8
N_MOD = 9


def _cparams(sem):
    return pltpu.CompilerParams(dimension_semantics=sem, vmem_limit_bytes=VMEM_LIMIT)


def _rms(x, g):
    ms = jnp.mean(x * x, axis=-1, keepdims=True)
    return x * lax.rsqrt(ms + EPS) * g


def _silu(x):
    return x * (1.0 / (1.0 + jnp.exp(-x)))


def _split3(v):
    hi = v.astype(BF16)
    r1 = v - hi.astype(F32)
    mid = r1.astype(BF16)
    lo = (r1 - mid.astype(F32)).astype(BF16)
    return hi, mid, lo


def _dot(a, b):
    return jnp.dot(a, b, preferred_element_type=F32)


def _mod_kernel(c_ref, w_ref, b_ref, o_ref):
    k = pl.program_id(1)
    s = _silu(c_ref[...]).astype(BF16)
    part = _dot(s, w_ref[0].astype(BF16))

    @pl.when(k == 0)
    def _():
        o_ref[0] = part + b_ref[0]

    @pl.when(k > 0)
    def _():
        o_ref[0] += part


def _mod_call(cvec, w_mod, b_mod):
    depth, d, n = w_mod.shape
    kc = 256
    return pl.pallas_call(
        _mod_kernel,
        grid=(depth, d // kc),
        in_specs=[
            pl.BlockSpec((8, kc), lambda l, k: (0, k)),
            pl.BlockSpec((1, kc, n), lambda l, k: (l, k, 0)),
            pl.BlockSpec((1, 1, n), lambda l, k: (l, 0, 0)),
        ],
        out_specs=pl.BlockSpec((1, 8, n), lambda l, k: (l, 0, 0)),
        out_shape=jax.ShapeDtypeStruct((depth, 8, n), F32),
        compiler_params=_cparams(("arbitrary", "arbitrary")),
        name="mod",
    )(cvec, w_mod, b_mod.reshape(depth, 1, n))


def _prenorm_kernel(xa_ref, xb_ref, mod_ref, ng_ref, x_ref, h_ref, *, n_a):
    def emit(x):
        x_ref[...] = x
        h = _rms(x, ng_ref[0:1, :]) * (1.0 + mod_ref[0, 1:2, :]) + mod_ref[0, 0:1, :]
        h_ref[...] = h.astype(BF16)

    @pl.when(pl.program_id(0) < n_a)
    def _():
        emit(xa_ref[...])

    @pl.when(pl.program_id(0) >= n_a)
    def _():
        emit(xb_ref[...])


def _prenorm_call(xa, xb, mod, ng, rows_per_group):
    d = xa.shape[1]
    m = xa.shape[0] + xb.shape[0]
    tm = 512
    n_a = xa.shape[0] // tm
    assert xa.shape[0] % tm == 0 and xb.shape[0] % tm == 0
    row = pl.BlockSpec((tm, d), lambda i: (i, 0))
    return pl.pallas_call(
        functools.partial(_prenorm_kernel, n_a=n_a),
        grid=(m // tm,),
        in_specs=[
            pl.BlockSpec((tm, d), lambda i: (jnp.minimum(i, n_a - 1), 0)),
            pl.BlockSpec((tm, d), lambda i: (jnp.maximum(i - n_a, 0), 0)),
            pl.BlockSpec((1, N_MOD, d), lambda i: (i * tm // rows_per_group, 0, 0)),
            pl.BlockSpec(ng.shape, lambda i: (0, 0)),
        ],
        out_specs=[row, row],
        out_shape=[jax.ShapeDtypeStruct((m, d), F32), jax.ShapeDtypeStruct((m, d), BF16)],
        compiler_params=_cparams(("arbitrary",)),
        name="prenorm",
    )(xa, xb, mod, ng)


def _ffn_kernel(h_ref, x_ref, wg_ref, wu_ref, wd_ref, mod_ref, ng_ref, modn_ref, ngn_ref,
                *rest, gate, ng_out, ng_next, sh_next, sc_next, emit_next, split_tile):
    if emit_next:
        xo_ref, ho_ref, acc_ref, act_ref = rest
        xo_refs = (xo_ref,)
    else:
        xo_a_ref, xo_b_ref, acc_ref, act_ref = rest
        xo_refs = (xo_a_ref, xo_b_ref)
    i = pl.program_id(0)
    j = pl.program_id(1)
    nf = pl.num_programs(1) - 2
    tm, d_model = acc_ref.shape
    th = tm // 2
    n_split = 4
    cw = d_model // n_split

    @pl.when(j == 0)
    def _():
        h = h_ref[...]
        act_ref[0] = (_silu(_dot(h, wg_ref[...])) * _dot(h, wu_ref[...])).astype(BF16)
        acc_ref[...] = jnp.zeros(acc_ref.shape, F32)

    @pl.when((j > 0) & (j < nf))
    def _():
        h = h_ref[...]
        g = _dot(h, wg_ref[...])
        u = _dot(h, wu_ref[...])
        a_prev = act_ref[(j + 1) % 2]
        for cblk in range(n_split):
            cols = slice(cblk * cw, (cblk + 1) * cw)
            acc_ref[:, cols] += _dot(a_prev, wd_ref[:, cols])
        act_ref[j % 2] = (_silu(g) * u).astype(BF16)

    def finish(y):
        xn = x_ref[...] + 0.5 * mod_ref[0, gate:gate + 1, :] * _rms(y, ng_ref[ng_out:ng_out + 1, :])
        if emit_next:
            xo_ref[...] = xn
            hn = (_rms(xn, ngn_ref[ng_next:ng_next + 1, :])
                  * (1.0 + modn_ref[0, sc_next:sc_next + 1, :]) + modn_ref[0, sh_next:sh_next + 1, :])
            ho_ref[...] = hn.astype(BF16)
        else:
            @pl.when(i < split_tile)
            def _():
                xo_refs[0][...] = xn

            @pl.when(i >= split_tile)
            def _():
                xo_refs[1][...] = xn

    @pl.when(j == nf)
    def _():
        a_prev = act_ref[(j + 1) % 2]
        acc_ref[th:, :] += _dot(a_prev[th:, :], wd_ref[...])
        finish(acc_ref[:th, :] + _dot(a_prev[:th, :], wd_ref[...]))

    @pl.when(j == nf + 1)
    def _():
        finish(acc_ref[th:, :])


def _ffn_call(h, x, w_up, w_down, mod, ng, mod_next, ng_next_arr, rows_per_group, *, gate, ng_out,
              ng_next, sh_next, sc_next, emit_next, split_rows=0):
    m, d = x.shape
    f = w_down.shape[0]
    tm, fc = 1024, 512
    th = tm // 2
    nf = f // fc
    kern = functools.partial(_ffn_kernel, gate=gate, ng_out=ng_out, ng_next=ng_next,
                             sh_next=sh_next, sc_next=sc_next, emit_next=emit_next,
                             split_tile=split_rows // tm)
    half = lambda i, j: 2 * i + jnp.where(j > nf, 1, 0)
    row_h = pl.BlockSpec((tm, d), lambda i, j: (i, 0))
    row_x = pl.BlockSpec((th, d), lambda i, j: (half(i, j), 0))
    modspec = pl.BlockSpec((1, N_MOD, d), lambda i, j: (i * tm // rows_per_group, 0, 0))
    ngspec = pl.BlockSpec(ng.shape, lambda i, j: (0, 0))
    if emit_next:
        out_specs = [row_x, row_x]
        out_shape = [jax.ShapeDtypeStruct((m, d), F32), jax.ShapeDtypeStruct((m, d), BF16)]
    else:
        assert split_rows % tm == 0
        na = split_rows // th
        out_specs = [pl.BlockSpec((th, d), lambda i, j: (jnp.minimum(half(i, j), na - 1), 0)),
                     pl.BlockSpec((th, d), lambda i, j: (jnp.maximum(half(i, j) - na, 0), 0))]
        out_shape = [jax.ShapeDtypeStruct((split_rows, d), F32),
                     jax.ShapeDtypeStruct((m - split_rows, d), F32)]
    return pl.pallas_call(
        kern,
        grid=(m // tm, nf + 2),
        in_specs=[
            row_h, row_x,
            pl.BlockSpec((d, fc), lambda i, j: (0, jnp.minimum(j, nf - 1))),
            pl.BlockSpec((d, fc), lambda i, j: (0, jnp.minimum(j, nf - 1) + nf)),
            pl.BlockSpec((fc, d), lambda i, j: (jnp.clip(j - 1, 0, nf - 1), 0)),
            modspec, ngspec, modspec, ngspec,
        ],
        out_specs=out_specs,
        out_shape=out_shape,
        scratch_shapes=[pltpu.VMEM((tm, d), F32), pltpu.VMEM((2, tm, fc), BF16)],
        compiler_params=_cparams(("arbitrary", "arbitrary")),
        name="ffn",
    )(h, x, w_up, w_up, w_down, mod, ng, mod_next, ng_next_arr)


def _z_kernel(h_ref, w_ref, o_ref):
    o_ref[...] = _silu(_dot(h_ref[...], w_ref[0])).astype(BF16)


def _z_call(h, w_in, layer, col0, n):
    m, d = h.shape
    tm, tn = 1024, 1024
    assert col0 % tn == 0 and n % tn == 0
    return pl.pallas_call(
        _z_kernel,
        grid=(m // tm, n // tn),
        in_specs=[pl.BlockSpec((tm, d), lambda i, j: (i, 0)),
                  pl.BlockSpec((1, d, tn), lambda i, j: (layer, 0, col0 // tn + j))],
        out_specs=pl.BlockSpec((tm, tn), lambda i, j: (i, j)),
        out_shape=jax.ShapeDtypeStruct((m, n), BF16),
        compiler_params=_cparams(("arbitrary", "arbitrary")),
        name="inproj_z",
    )(h, w_in)


HALO = 8


def _xbc_kernel(h_ref, w_ref, cw_ref, cb_ref, o_ref, s_ref, t_ref, *, ctx_rows, ctx_seg, lat_seg):
    tm, tn = o_ref.shape
    i = pl.program_id(0)
    half = CONV_K // 2
    cwid = 2 * LANES
    per = ctx_seg // lat_seg
    nv = lat_seg // 8
    is_lat = i * tm >= ctx_rows
    sub = lax.broadcasted_iota(jnp.int32, (8, LANES), 0)

    @pl.when((i == 0) & (pl.program_id(1) == 0))
    def _():
        for c in range(tn // LANES):
            s_ref[c, 0:HALO, :] = jnp.zeros((HALO, LANES), F32)
            s_ref[c, HALO + tm:2 * HALO + tm, :] = jnp.zeros((HALO, LANES), F32)

    def pad_select(v, crosses, static_boundary):
        if static_boundary:
            return jnp.where(crosses, 0.0, v)
        return jnp.where(jnp.logical_and(crosses, is_lat), 0.0, v)

    for nb in range(tn // cwid):
        acc = _dot(h_ref[...], w_ref[0, :, nb * cwid:(nb + 1) * cwid])
        for c2 in range(cwid // LANES):
            s_ref[nb * (cwid // LANES) + c2, HALO:HALO + tm, :] = acc[:, c2 * LANES:(c2 + 1) * LANES]
        for c2 in range(cwid // LANES):
            c = nb * (cwid // LANES) + c2
            lanes = slice(c * LANES, (c + 1) * LANES)
            wts = [cw_ref[k:k + 1, lanes] for k in range(CONV_K)]
            bias = cb_ref[:, lanes]
            for g in range(tm // lat_seg):
                r0 = HALO + g * lat_seg
                vec = {}
                for e in range(-half, nv + half):
                    v = s_ref[c, pl.ds(r0 + e, 8, stride=8), :]
                    if e < 0:
                        v = pad_select(v, sub == 0, g % per == 0)
                    if e >= nv:
                        v = pad_select(v, sub == 7, g % per == per - 1)
                    vec[e] = v
                for a in range(nv):
                    out = bias
                    for dlt in range(-half, half + 1):
                        out = out + vec[a + dlt] * wts[half + dlt]
                    t_ref[c, pl.ds(g * lat_seg + a, 8, stride=8), :] = _silu(out)
            o_ref[:, lanes] = t_ref[c].astype(BF16)


def _xbc_call(h, w_in, layer, col0, conv_w, conv_b, ctx_rows, ctx_seg):
    m, d = h.shape
    n = conv_w.shape[1]
    tm, tn = 512, 1024
    assert col0 % tn == 0 and n % tn == 0
    assert tm % ctx_seg == 0 and ctx_seg % GRID_W == 0 and ctx_rows % tm == 0
    assert GRID_W == 64 and CONV_K // 2 <= HALO
    kern = functools.partial(_xbc_kernel, ctx_rows=ctx_rows, ctx_seg=ctx_seg, lat_seg=GRID_W)
    stage_rows = tm + 2 * HALO
    return pl.pallas_call(
        kern,
        grid=(m // tm, n // tn),
        in_specs=[pl.BlockSpec((tm, d), lambda i, j: (i, 0)),
                  pl.BlockSpec((1, d, tn), lambda i, j: (layer, 0, col0 // tn + j)),
                  pl.BlockSpec((CONV_K, tn), lambda i, j: (0, j)),
                  pl.BlockSpec((1, tn), lambda i, j: (0, j))],
        out_specs=pl.BlockSpec((tm, tn), lambda i, j: (i, j)),
        out_shape=jax.ShapeDtypeStruct((m, n), BF16),
        scratch_shapes=[pltpu.VMEM((tn // LANES, stage_rows, LANES), F32),
                        pltpu.VMEM((tn // LANES, tm, LANES), F32)],
        compiler_params=_cparams(("arbitrary", "arbitrary")),
        name="inproj_xbc",
    )(h, w_in, conv_w, conv_b.reshape(1, n))


def _dt_kernel(h_ref, w_ref, b_ref, o_ref):
    v = _dot(h_ref[...], w_ref[0]) + b_ref[...]
    o_ref[...] = jnp.maximum(v, 0.0) + jnp.log1p(jnp.exp(-jnp.abs(v)))


def _dt_call(h, w_in, layer, col0, b):
    m, d = h.shape
    n = LANES
    assert col0 % n == 0
    tm = 1024
    return pl.pallas_call(
        _dt_kernel,
        grid=(m // tm,),
        in_specs=[pl.BlockSpec((tm, d), lambda i: (i, 0)),
                  pl.BlockSpec((1, d, n), lambda i: (layer, 0, col0 // n)),
                  pl.BlockSpec((1, n), lambda i: (0, 0))],
        out_specs=pl.BlockSpec((tm, n), lambda i: (i, 0)),
        out_shape=jax.ShapeDtypeStruct((m, n), F32),
        compiler_params=_cparams(("arbitrary",)),
        name="inproj_dt",
    )(h, w_in, b)


def _fnet_in_kernel(h_ref, w_ref, cs_ref, xc_ref, xs_ref):
    f = _dot(h_ref[...], w_ref[...]).astype(BF16)
    gw = cs_ref.shape[0]
    for g in range(FNET_GROUPS):
        sl = slice(g * gw, (g + 1) * gw)
        r = _dot(f[:, sl], cs_ref[...])
        xc_ref[:, sl] = r[:, :gw].astype(BF16)
        xs_ref[:, sl] = r[:, gw:].astype(BF16)


def _fnet_in_call(h, w, cs):
    m, d = h.shape
    n = w.shape[1]
    tm = 512
    spec_o = pl.BlockSpec((tm, n), lambda i: (i, 0))
    return pl.pallas_call(
        _fnet_in_kernel,
        grid=(m // tm,),
        in_specs=[pl.BlockSpec((tm, d), lambda i: (i, 0)),
                  pl.BlockSpec((d, n), lambda i: (0, 0)),
                  pl.BlockSpec(cs.shape, lambda i: (0, 0))],
        out_specs=[spec_o, spec_o],
        out_shape=[jax.ShapeDtypeStruct((m, n), BF16)] * 2,
        compiler_params=_cparams(("arbitrary",)),
        name="inproj_fnet",
    )(h, w, cs)


def _fnet_pos_kernel(xc_c_ref, xs_c_ref, xc_l_ref, xs_l_ref, cc_ref, sc_ref, cl_ref, sl_ref, o_ref,
                     *, n_ctx_tiles):
    i = pl.program_id(0)

    @pl.when(i < n_ctx_tiles)
    def _():
        o_ref[...] = (_dot(cc_ref[...], xc_c_ref[...]) - _dot(sc_ref[...], xs_c_ref[...])).astype(BF16)

    @pl.when(i >= n_ctx_tiles)
    def _():
        o_ref[...] = (_dot(cl_ref[...], xc_l_ref[...]) - _dot(sl_ref[...], xs_l_ref[...])).astype(BF16)


def _fnet_pos_call(xc, xs, cc, sc, cl, sl, ctx_rows):
    m, n = xc.shape
    tm = cc.shape[0]
    lat_len = cl.shape[0]
    n_ctx = ctx_rows // tm
    per_lat = lat_len // tm
    ctx_spec = pl.BlockSpec((tm, n), lambda i: (jnp.minimum(i, n_ctx - 1), 0))
    lat_spec = pl.BlockSpec(
        (lat_len, n), lambda i: (ctx_rows // lat_len + jnp.maximum(i - n_ctx, 0) // per_lat, 0))
    dft_c = pl.BlockSpec((tm, tm), lambda i: (0, 0))
    dft_l = pl.BlockSpec((tm, lat_len), lambda i: (jnp.maximum(i - n_ctx, 0) % per_lat, 0))
    kern = functools.partial(_fnet_pos_kernel, n_ctx_tiles=n_ctx)
    return pl.pallas_call(
        kern,
        grid=(m // tm,),
        in_specs=[ctx_spec, ctx_spec, lat_spec, lat_spec, dft_c, dft_c, dft_l, dft_l],
        out_specs=pl.BlockSpec((tm, n), lambda i: (i, 0)),
        out_shape=jax.ShapeDtypeStruct((m, n), BF16),
        compiler_params=_cparams(("arbitrary",)),
        name="fnet_pos",
    )(xc, xs, xc, xs, cc, sc, cl, sl)


def _sgu_kernel(h_ref, w_ref, g_ref, wsp_ref, bsp_ref, o_ref):
    tm, width = o_ref.shape
    hd = width // SGU_HEADS
    acc = _dot(h_ref[...], w_ref[...])
    k0 = math.sqrt(2.0 / math.pi)
    s = 0.5 * acc * (1.0 + jnp.tanh(k0 * (acc + 0.044715 * (acc * acc * acc))))
    u = s[:, :width]
    v = _rms(s[:, width:], g_ref[...]).astype(BF16)
    for k in range(tm // CHUNK):
        rows = slice(k * CHUNK, (k + 1) * CHUNK)
        for hh in range(SGU_HEADS):
            cols = slice(hh * hd, (hh + 1) * hd)
            sp = _dot(wsp_ref[hh], v[rows, cols]) + bsp_ref[:, hh:hh + 1]
            o_ref[rows, cols] = (u[rows, cols] * sp).astype(BF16)


def _sgu_call(h, w, g_sgu, w_sp, b_sp_t):
    m, d = h.shape
    n = w.shape[1]
    width = n // 2
    tm = 512
    return pl.pallas_call(
        _sgu_kernel,
        grid=(m // tm,),
        in_specs=[pl.BlockSpec((tm, d), lambda i: (i, 0)),
                  pl.BlockSpec((d, n), lambda i: (0, 0)),
                  pl.BlockSpec((1, width), lambda i: (0, 0)),
                  pl.BlockSpec(w_sp.shape, lambda i: (0, 0, 0)),
                  pl.BlockSpec(b_sp_t.shape, lambda i: (0, 0))],
        out_specs=pl.BlockSpec((tm, width), lambda i: (i, 0)),
        out_shape=jax.ShapeDtypeStruct((m, width), BF16),
        compiler_params=_cparams(("arbitrary",)),
        name="inproj_sgu",
    )(h, w, g_sgu.reshape(1, width), w_sp, b_sp_t)


def _ssd_kernel(tbl_ref, xs_ref, b_ref, c_ref, dt_ref, h0_ref, e3_ref, tri3_ref, alog_ref, *rest,
                reverse, final, n_ctx_seq, lane_off):
    if final:
        yprev_ref, zs_ref, dsk_ref, gssd_ref, y_ref, hfin_ref, st_ref, ybuf_ref = rest
    else:
        y_ref, hfin_ref, st_ref = rest
    s = pl.program_id(0)
    seq = tbl_ref[1, s]
    first = tbl_ref[2, s]
    last = tbl_ref[3, s]
    T = CHUNK
    hp = xs_ref.shape[1]
    gw = hp // SSD_GROUPS
    n_heads = hp // SSD_HEAD_DIM

    @pl.when(first == 1)
    def _():
        h0 = h0_ref[0, 0, 0].reshape(hp, SSD_STATE)
        h0 = jnp.where(seq < n_ctx_seq, 0.0, h0)
        st_ref[...] = h0.T

    row = lax.broadcasted_iota(jnp.int32, (T, T), 0)
    col = lax.broadcasted_iota(jnp.int32, (T, T), 1)
    tri = (row <= col) if reverse else (row >= col)

    dt = dt_ref[...]
    a = dt * (-jnp.exp(alog_ref[...]))
    acs = _dot(tri3_ref[...], jnp.concatenate(_split3(a), axis=0))
    end = acs[0:1, :] if reverse else acs[T - 1:T, :]
    lr_t = (acs - jnp.log(dt)).T
    w_t = (jnp.exp(end - acs) * dt).T
    ex = _dot(jnp.concatenate(_split3(jnp.exp(acs)), axis=1), e3_ref[...])
    dec = ex[0:1, :] if reverse else ex[T - 1:T, :]

    lane =lax.broadcasted_iota(jnp.int32, (1, LANES), 1)
    low_bf = jnp.where(lane < SSD_HEAD_DIM, 1.0, 0.0).astype(BF16)
    high_bf = jnp.where(lane < SSD_HEAD_DIM, 0.0, 1.0).astype(BF16)
    if final:
        dsum = dsk_ref[0:1, :] + dsk_ref[1:2, :]
        sq_acc = jnp.zeros((T, LANES), F32)

    heads_per_group = n_heads // SSD_GROUPS
    for g in range(SSD_GROUPS):
        bg = b_ref[:, g * SSD_STATE:(g + 1) * SSD_STATE]
        cg = c_ref[:, g * SSD_STATE:(g + 1) * SSD_STATE]
        gs = slice(g * gw, (g + 1) * gw)
        cb = lax.dot_general(cg, bg, (((1,), (1,)), ((), ())), preferred_element_type=F32)
        b_t = bg.astype(F32).T
        yoff = _dot(cg, st_ref[:, gs].astype(BF16))
        for q in range(heads_per_group // 2):
            ms, bs = [], []
            for r in (2 * q, 2 * q + 1):
                hl = lane_off + g * heads_per_group + r
                diff = acs[:, hl:hl + 1] - lr_t[hl:hl + 1, :]
                ms.append((cb * jnp.exp(jnp.where(tri, diff, -jnp.inf))).astype(BF16))
                bs.append((b_t * w_t[hl:hl + 1, :]).astype(BF16))
            c0 = g * gw + q * LANES
            cs_ = slice(c0, c0 + LANES)
            xpair = xs_ref[:, cs_]
            rhs = jnp.concatenate([xpair * low_bf, xpair * high_bf], axis=0)
            yd = _dot(jnp.concatenate(ms, axis=1), rhs)
            st_ref[:, cs_] = st_ref[:, cs_] * dec[:, cs_] + _dot(jnp.concatenate(bs, axis=1), rhs)
            yblk = yd + yoff[:, q * LANES:(q + 1) * LANES] * ex[:, cs_]
            if final:
                t = ((yblk + yprev_ref[:, cs_].astype(F32) + xpair.astype(F32) * dsum[:, cs_])
                     * zs_ref[:, cs_].astype(F32))
                sq_acc = sq_acc + t * t
                ybuf_ref[:, cs_] = t
            else:
                y_ref[:, cs_] = yblk.astype(BF16)

    if final:
        scale = lax.rsqrt(jnp.sum(sq_acc, axis=-1, keepdims=True) / hp + EPS)
        y_ref[...] = (ybuf_ref[...] * scale * gssd_ref[...]).astype(BF16)

    @pl.when(last == 1)
    def _():
        hfin_ref[0] = st_ref[...].T


def _ssd_tables(n_ctx_seq, ctx_chunks, n_lat_seq, lat_chunks, reverse):
    blk, seq, first, last = [], [], [], []
    base = 0
    for sidx, nch in [(i, ctx_chunks) for i in range(n_ctx_seq)] + \
                     [(n_ctx_seq + i, lat_chunks) for i in range(n_lat_seq)]:
        order = list(range(nch))[::-1] if reverse else list(range(nch))
        for k, cidx in enumerate(order):
            blk.append(base + cidx)
            seq.append(sidx)
            first.append(1 if k == 0 else 0)
            last.append(1 if k == nch - 1 else 0)
        base += nch
    return np.array([blk, seq, first, last], dtype=np.int32)


def _ssd_call(xbc, dt, state_ssd, layer, e3_mat, tri3, alog128, geom, *, reverse, final, extra=()):
    n_ctx_seq, ctx_chunks, n_lat_seq, lat_chunks = geom
    m = xbc.shape[0]
    hp = e3_mat.shape[1]
    n_groups_cols = SSD_GROUPS * SSD_STATE
    tbl = jnp.asarray(_ssd_tables(n_ctx_seq, ctx_chunks, n_lat_seq, lat_chunks, reverse))
    steps = tbl.shape[1]
    n_seq = n_ctx_seq + n_lat_seq
    direction = 1 if reverse else 0
    n_heads = hp // SSD_HEAD_DIM
    kern = functools.partial(_ssd_kernel, reverse=reverse, final=final, n_ctx_seq=n_ctx_seq,
                             lane_off=direction * n_heads)
    rowblk = lambda width, cblk: pl.BlockSpec((CHUNK, width), lambda s, t: (t[0, s], cblk))
    const2 = lambda shape: pl.BlockSpec(shape, lambda s, t: (0, 0))
    in_specs = [
        rowblk(hp, 0),
        rowblk(n_groups_cols, hp // n_groups_cols),
        rowblk(n_groups_cols, hp // n_groups_cols + 1),
        rowblk(LANES, 0),
        pl.BlockSpec((1, 1, 1) + state_ssd.shape[3:],
                     lambda s, t: (jnp.maximum(t[1, s] - n_ctx_seq, 0), layer, direction, 0, 0, 0)),
        const2(e3_mat.shape), const2(tri3.shape), const2(alog128.shape),
    ]
    args = [xbc, xbc, xbc, dt, state_ssd, e3_mat, tri3, alog128]
    scratch = [pltpu.VMEM((SSD_STATE, hp), F32)]
    if final:
        yprev, zs, dsk, gssd = extra
        in_specs += [rowblk(hp, 0), rowblk(hp, 0), const2(dsk.shape), const2(gssd.shape)]
        args += [yprev, zs, dsk, gssd]
        scratch.append(pltpu.VMEM((CHUNK, hp), F32))
    y_dtype = BF16
    grid_spec = pltpu.PrefetchScalarGridSpec(
        num_scalar_prefetch=1,
        grid=(steps,),
        in_specs=in_specs,
        out_specs=[rowblk(hp, 0),
                   pl.BlockSpec((1, hp, SSD_STATE), lambda s, t: (t[1, s], 0, 0))],
        scratch_shapes=scratch,
    )
    return pl.pallas_call(
        kern,
        grid_spec=grid_spec,
        out_shape=[jax.ShapeDtypeStruct((m, hp), y_dtype),
                   jax.ShapeDtypeStruct((n_seq, hp, SSD_STATE), F32)],
        compiler_params=_cparams(("arbitrary",)),
        name="ssd_bwd" if reverse else "ssd_fwd",
    )(tbl, *args)


def _mixout_kernel(ys_ref, yf_ref, yg_ref, x_ref, w_ref, mod_ref, ng_ref, xo_ref, ho_ref):
    ks = ys_ref.shape[1]
    kf = yf_ref.shape[1]
    tm = xo_ref.shape[0]
    n_sub = 2
    for r in range(n_sub):
        rows = slice(r * (tm // n_sub), (r + 1) * (tm // n_sub))
        mm = (_dot(ys_ref[rows, :], w_ref[:ks, :]) + _dot(yf_ref[rows, :], w_ref[ks:ks + kf, :])
              + _dot(yg_ref[rows, :], w_ref[ks + kf:, :]))
        xn = x_ref[rows, :] + mod_ref[0, 5:6, :] * _rms(mm, ng_ref[3:4, :])
        xo_ref[rows, :] = xn
        hn = _rms(xn, ng_ref[4:5, :]) * (1.0 + mod_ref[0, 7:8, :]) + mod_ref[0, 6:7, :]
        ho_ref[rows, :] = hn.astype(BF16)


def _mixout_call(yssd, yfnet, ysgu, x, w_out, mod, ng, rows_per_group):
    m, d = x.shape
    assert yssd.shape[1] + yfnet.shape[1] + ysgu.shape[1] == w_out.shape[0]
    tm = 512
    row = lambda width: pl.BlockSpec((tm, width), lambda i: (i, 0))
    return pl.pallas_call(
        _mixout_kernel,
        grid=(m // tm,),
        in_specs=[row(yssd.shape[1]), row(yfnet.shape[1]), row(ysgu.shape[1]), row(d),
                  pl.BlockSpec(w_out.shape, lambda i: (0, 0), pipeline_mode=pl.Buffered(1)),
                  pl.BlockSpec((1, N_MOD, d), lambda i: (i * tm // rows_per_group, 0, 0)),
                  pl.BlockSpec(ng.shape, lambda i: (0, 0))],
        out_specs=[row(d), row(d)],
        out_shape=[jax.ShapeDtypeStruct((m, d), F32), jax.ShapeDtypeStruct((m, d), BF16)],
        compiler_params=_cparams(("arbitrary",)),
        name="mixout",
    )(yssd, yfnet, ysgu, x, w_out, mod, ng)


def _dft_tables(n):
    scale = 1.0 / math.sqrt(n)
    r = int(round(math.sqrt(n)))
    if r * r != n or n < 1024:
        j = lax.broadcasted_iota(jnp.int32, (n, n), 0)
        k = lax.broadcasted_iota(jnp.int32, (n, n), 1)
        ang = ((j * k) % n).astype(F32) * (2.0 * math.pi / n)
        return (jnp.cos(ang) * scale).astype(BF16), (jnp.sin(ang) * scale).astype(BF16)
    jj = lax.broadcasted_iota(jnp.int32, (r, n), 0)
    kk = lax.broadcasted_iota(jnp.int32, (r, n), 1)
    a1 = ((jj * r * kk) % n).astype(F32) * (2.0 * math.pi / n)
    a2 = ((jj * kk) % n).astype(F32) * (2.0 * math.pi / n)
    c1, s1 = (jnp.cos(a1) * scale)[:, None, :], (jnp.sin(a1) * scale)[:, None, :]
    c2, s2 = jnp.cos(a2)[None, :, :], jnp.sin(a2)[None, :, :]
    cos = (c1 * c2 - s1 * s2).reshape(n, n)
    sin = (s1 * c2 + c1 * s2).reshape(n, n)
    return cos.astype(BF16), sin.astype(BF16)


def kernel(x_prompt, x_sample, state_ssd, c, c_ctx, w_mod, b_mod, norm_g, w_ffn1_up, w_ffn1_down,
           w_in, conv_w, conv_b, a_log, dt_bias, d_skip, g_ssd, g_sgu, w_sp, b_sp, w_out,
           w_ffn2_up, w_ffn2_down):
    batch, seq, d = x_prompt.shape
    dec_batch, dec_seq, _ = x_sample.shape
    depth = w_mod.shape[0]
    ctx_rows = batch * seq
    assert ctx_rows == dec_seq, "row tiles map to modulation groups in units of dec_seq rows"
    n_heads = a_log.shape[2]
    hp = n_heads * SSD_HEAD_DIM
    gn = SSD_GROUPS * SSD_STATE
    off_dt = hp + hp + 2 * gn
    off_fnet = off_dt + 2 * n_heads
    fnet_w = w_out.shape[1] // 4
    off_sgu = off_fnet + fnet_w

    cvec = jnp.concatenate([c_ctx[None, :], c, jnp.zeros((8 - 1 - dec_batch, d), F32)], axis=0)
    mod_all = _mod_call(cvec, w_mod, b_mod).reshape(depth, 8, N_MOD, d)

    gw = fnet_w // FNET_GROUPS
    cc_g, sc_g = _dft_tables(gw)
    cs_chan = jnp.concatenate([cc_g, sc_g], axis=1)
    cc, sc = _dft_tables(seq)
    cl, sl = _dft_tables(dec_seq)
    e_np = np.zeros((2, LANES, hp), np.float32)
    for dr in range(2):
        for hh in range(n_heads):
            e_np[dr, dr * n_heads + hh, hh * SSD_HEAD_DIM:(hh + 1) * SSD_HEAD_DIM] = 1.0
    e3_mats = [jnp.asarray(np.tile(e_np[dr], (3, 1)), BF16) for dr in range(2)]
    tidx = np.arange(CHUNK)
    tri3s = [jnp.asarray(np.tile((tidx[:, None] >= tidx[None, :]).astype(np.float32), (1, 3)), BF16),
             jnp.asarray(np.tile((tidx[:, None] <= tidx[None, :]).astype(np.float32), (1, 3)), BF16)]
    geom = (batch, seq // CHUNK, dec_batch, dec_seq // CHUNK)

    x, h = _prenorm_call(x_prompt.reshape(ctx_rows, d), x_sample.reshape(dec_batch * dec_seq, d),
                         mod_all[0], norm_g[0], dec_seq)
    w_in_bf = w_in.astype(BF16)
    states = []
    for l in range(depth):
        mod = mod_all[l]
        ng = norm_g[l]
        w_f = w_in_bf[l, :, off_fnet:off_sgu]
        w_s = w_in_bf[l, :, off_sgu:]
        dtb = jnp.pad(dt_bias[l].reshape(1, 2 * n_heads), ((0, 0), (0, LANES - 2 * n_heads)))
        alog128 = jnp.pad(a_log[l].reshape(1, 2 * n_heads), ((0, 0), (0, LANES - 2 * n_heads)))
        dskx = jnp.repeat(d_skip[l], SSD_HEAD_DIM, axis=1)

        x, h = _ffn_call(h, x, w_ffn1_up[l].astype(BF16), w_ffn1_down[l].astype(BF16), mod, ng, mod, ng,
                         dec_seq, gate=2, ng_out=1, ng_next=2, sh_next=3, sc_next=4, emit_next=True)

        zs = _z_call(h, w_in_bf, l, 0, hp)
        xbc = _xbc_call(h, w_in_bf, l, hp, conv_w[l], conv_b[l], ctx_rows, seq)
        dt = _dt_call(h, w_in_bf, l, off_dt, dtb)
        xc, xsn = _fnet_in_call(h, w_f, cs_chan)
        ysgu = _sgu_call(h, w_s, g_sgu[l], w_sp[l].astype(BF16), b_sp[l].T)

        y_f, hfin_f = _ssd_call(xbc, dt, state_ssd, l, e3_mats[0], tri3s[0], alog128, geom,
                                reverse=False, final=False)
        yssd, hfin_b = _ssd_call(xbc, dt, state_ssd, l, e3_mats[1], tri3s[1], alog128, geom,
                                 reverse=True, final=True,
                                 extra=(y_f, zs, dskx, g_ssd[l].reshape(1, hp)))
        states.append(jnp.stack([hfin_f[:batch], hfin_b[:batch]], axis=1))

        yfnet = _fnet_pos_call(xc, xsn, cc, sc, cl, sl, ctx_rows)
        x, h = _mixout_call(yssd, yfnet, ysgu, x, w_out[l].astype(BF16), mod, ng, dec_seq)

        last = l == depth - 1
        nxt = l if last else l + 1
        res = _ffn_call(h, x, w_ffn2_up[l].astype(BF16), w_ffn2_down[l].astype(BF16), mod, ng,
                        mod_all[nxt], norm_g[nxt], dec_seq, gate=8, ng_out=5, ng_next=0,
                        sh_next=0, sc_next=1, emit_next=not last, split_rows=ctx_rows)
        if not last:
            x, h = res

    y_prompt = res[0].reshape(batch, seq, d)
    y_sample = res[1].reshape(dec_batch, dec_seq, d)
    new_state = jnp.stack(states, axis=1).reshape(
        batch, depth, 2, n_heads, SSD_HEAD_DIM, SSD_STATE).astype(x_prompt.dtype)
    return (y_prompt, y_sample, new_state)
```

```python
import functools
import math

import numpy as np
import jax
import jax.numpy as jnp
from jax import lax
from jax.experimental import pallas as pl
from jax.experimental.pallas import tpu as pltpu

F32 = jnp.float32
BF16 = jnp.bfloat16
EPS = 1e-6

LANES = 128
VMEM_LIMIT = 62 * 1024 * 1024

SSD_HEAD_DIM = 64
SSD_GROUPS = 4
SSD_STATE = 128
CHUNK = 128
CONV_K = 5
GRID_W = 64
FNET_GROUPS = 4
SGU_HEADS = 8
N_MOD = 9


def _cparams(sem):
    return pltpu.CompilerParams(dimension_semantics=sem, vmem_limit_bytes=VMEM_LIMIT)


def _rms(x, g):
    ms = jnp.mean(x * x, axis=-1, keepdims=True)
    return x * lax.rsqrt(ms + EPS) * g


def _silu(x):
    return x * (1.0 / (1.0 + jnp.exp(-x)))


def _split3(v):
    hi = v.astype(BF16)
    r1 = v - hi.astype(F32)
    mid = r1.astype(BF16)
    lo = (r1 - mid.astype(F32)).astype(BF16)
    return hi, mid, lo


def _dot(a, b):
    return jnp.dot(a, b, preferred_element_type=F32)


def _mod_kernel(c_ref, w_ref, b_ref, o_ref):
    k = pl.program_id(1)
    s = _silu(c_ref[...]).astype(BF16)
    part = _dot(s, w_ref[0].astype(BF16))

    @pl.when(k == 0)
    def _():
        o_ref[0] = part + b_ref[0]

    @pl.when(k > 0)
    def _():
        o_ref[0] += part


def _mod_call(cvec, w_mod, b_mod):
    depth, d, n = w_mod.shape
    kc = 256
    return pl.pallas_call(
        _mod_kernel,
        grid=(depth, d // kc),
        in_specs=[
            pl.BlockSpec((8, kc), lambda l, k: (0, k)),
            pl.BlockSpec((1, kc, n), lambda l, k: (l, k, 0)),
            pl.BlockSpec((1, 1, n), lambda l, k: (l, 0, 0)),
        ],
        out_specs=pl.BlockSpec((1, 8, n), lambda l, k: (l, 0, 0)),
        out_shape=jax.ShapeDtypeStruct((depth, 8, n), F32),
        compiler_params=_cparams(("arbitrary", "arbitrary")),
        name="mod",
    )(cvec, w_mod, b_mod.reshape(depth, 1, n))


def _prenorm_kernel(xa_ref, xb_ref, mod_ref, ng_ref, x_ref, h_ref, *, n_a):
    def emit(x):
        x_ref[...] = x
        h = _rms(x, ng_ref[0:1, :]) * (1.0 + mod_ref[0, 1:2, :]) + mod_ref[0, 0:1, :]
        h_ref[...] = h.astype(BF16)

    @pl.when(pl.program_id(0) < n_a)
    def _():
        emit(xa_ref[...])

    @pl.when(pl.program_id(0) >= n_a)
    def _():
        emit(xb_ref[...])


def _prenorm_call(xa, xb, mod, ng, rows_per_group):
    d = xa.shape[1]
    m = xa.shape[0] + xb.shape[0]
    tm = 512
    n_a = xa.shape[0] // tm
    assert xa.shape[0] % tm == 0 and xb.shape[0] % tm == 0
    row = pl.BlockSpec((tm, d), lambda i: (i, 0))
    return pl.pallas_call(
        functools.partial(_prenorm_kernel, n_a=n_a),
        grid=(m // tm,),
        in_specs=[
            pl.BlockSpec((tm, d), lambda i: (jnp.minimum(i, n_a - 1), 0)),
            pl.BlockSpec((tm, d), lambda i: (jnp.maximum(i - n_a, 0), 0)),
            pl.BlockSpec((1, N_MOD, d), lambda i: (i * tm // rows_per_group, 0, 0)),
            pl.BlockSpec(ng.shape, lambda i: (0, 0)),
        ],
        out_specs=[row, row],
        out_shape=[jax.ShapeDtypeStruct((m, d), F32), jax.ShapeDtypeStruct((m, d), BF16)],
        compiler_params=_cparams(("arbitrary",)),
        name="prenorm",
    )(xa, xb, mod, ng)


def _ffn_kernel(h_ref, x_ref, wg_ref, wu_ref, wd_ref, mod_ref, ng_ref, modn_ref, ngn_ref,
                *rest, gate, ng_out, ng_next, sh_next, sc_next, emit_next, split_tile):
    if emit_next:
        xo_ref, ho_ref, acc_ref, act_ref = rest
        xo_refs = (xo_ref,)
    else:
        xo_a_ref, xo_b_ref, acc_ref, act_ref = rest
        xo_refs = (xo_a_ref, xo_b_ref)
    i = pl.program_id(0)
    j = pl.program_id(1)
    nf = pl.num_programs(1) - 2
    tm, d_model = acc_ref.shape
    th = tm // 2
    n_split = 4
    cw = d_model // n_split

    @pl.when(j == 0)
    def _():
        h = h_ref[...]
        act_ref[0] = (_silu(_dot(h, wg_ref[...])) * _dot(h, wu_ref[...])).astype(BF16)
        acc_ref[...] = jnp.zeros(acc_ref.shape, F32)

    @pl.when((j > 0) & (j < nf))
    def _():
        h = h_ref[...]
        g = _dot(h, wg_ref[...])
        u = _dot(h, wu_ref[...])
        a_prev = act_ref[(j + 1) % 2]
        for cblk in range(n_split):
            cols = slice(cblk * cw, (cblk + 1) * cw)
            acc_ref[:, cols] += _dot(a_prev, wd_ref[:, cols])
        act_ref[j % 2] = (_silu(g) * u).astype(BF16)

    def finish(y):
        xn = x_ref[...] + 0.5 * mod_ref[0, gate:gate + 1, :] * _rms(y, ng_ref[ng_out:ng_out + 1, :])
        if emit_next:
            xo_ref[...] = xn
            hn = (_rms(xn, ngn_ref[ng_next:ng_next + 1, :])
                  * (1.0 + modn_ref[0, sc_next:sc_next + 1, :]) + modn_ref[0, sh_next:sh_next + 1, :])
            ho_ref[...] = hn.astype(BF16)
        else:
            @pl.when(i < split_tile)
            def _():
                xo_refs[0][...] = xn

            @pl.when(i >= split_tile)
            def _():
                xo_refs[1][...] = xn

    @pl.when(j == nf)
    def _():
        a_prev = act_ref[(j + 1) % 2]
        acc_ref[th:, :] += _dot(a_prev[th:, :], wd_ref[...])
        finish(acc_ref[:th, :] + _dot(a_prev[:th, :], wd_ref[...]))

    @pl.when(j == nf + 1)
    def _():
        finish(acc_ref[th:, :])


def _ffn_call(h, x, w_up, w_down, layer, mod, ng, mod_next, ng_next_arr, rows_per_group, *, gate,
              ng_out, ng_next, sh_next, sc_next, emit_next, split_rows=0):
    m, d = x.shape
    f = w_down.shape[1]
    tm, fc = 1024, 512
    th = tm // 2
    nf = f // fc
    kern = functools.partial(_ffn_kernel, gate=gate, ng_out=ng_out, ng_next=ng_next,
                             sh_next=sh_next, sc_next=sc_next, emit_next=emit_next,
                             split_tile=split_rows // tm)
    half = lambda i, j: 2 * i + jnp.where(j > nf, 1, 0)
    row_h = pl.BlockSpec((tm, d), lambda i, j: (i, 0))
    row_x = pl.BlockSpec((th, d), lambda i, j: (half(i, j), 0))
    modspec = pl.BlockSpec((1, N_MOD, d), lambda i, j: (i * tm // rows_per_group, 0, 0))
    ngspec = pl.BlockSpec(ng.shape, lambda i, j: (0, 0))
    if emit_next:
        out_specs = [row_x, row_x]
        out_shape = [jax.ShapeDtypeStruct((m, d), F32), jax.ShapeDtypeStruct((m, d), BF16)]
    else:
        assert split_rows % tm == 0
        na = split_rows // th
        out_specs = [pl.BlockSpec((th, d), lambda i, j: (jnp.minimum(half(i, j), na - 1), 0)),
                     pl.BlockSpec((th, d), lambda i, j: (jnp.maximum(half(i, j) - na, 0), 0))]
        out_shape = [jax.ShapeDtypeStruct((split_rows, d), F32),
                     jax.ShapeDtypeStruct((m - split_rows, d), F32)]
    return pl.pallas_call(
        kern,
        grid=(m // tm, nf + 2),
        in_specs=[
            row_h, row_x,
            pl.BlockSpec((None, d, fc), lambda i, j: (layer, 0, jnp.minimum(j, nf - 1))),
            pl.BlockSpec((None, d, fc), lambda i, j: (layer, 0, jnp.minimum(j, nf - 1) + nf)),
            pl.BlockSpec((None, fc, d), lambda i, j: (layer, jnp.clip(j - 1, 0, nf - 1), 0)),
            modspec, ngspec, modspec, ngspec,
        ],
        out_specs=out_specs,
        out_shape=out_shape,
        scratch_shapes=[pltpu.VMEM((tm, d), F32), pltpu.VMEM((2, tm, fc), BF16)],
        compiler_params=_cparams(("arbitrary", "arbitrary")),
        name="ffn",
    )(h, x, w_up, w_up, w_down, mod, ng, mod_next, ng_next_arr)


def _z_kernel(h_ref, w_ref, o_ref):
    o_ref[...] = _silu(_dot(h_ref[...], w_ref[0])).astype(BF16)


def _z_call(h, w_in, layer, col0, n):
    m, d = h.shape
    tm, tn = 1024, 1024
    assert col0 % tn == 0 and n % tn == 0
    return pl.pallas_call(
        _z_kernel,
        grid=(m // tm, n // tn),
        in_specs=[pl.BlockSpec((tm, d), lambda i, j: (i, 0)),
                  pl.BlockSpec((1, d, tn), lambda i, j: (layer, 0, col0 // tn + j))],
        out_specs=pl.BlockSpec((tm, tn), lambda i, j: (i, j)),
        out_shape=jax.ShapeDtypeStruct((m, n), BF16),
        compiler_params=_cparams(("arbitrary", "arbitrary")),
        name="inproj_z",
    )(h, w_in)


HALO = 8


def _xbc_kernel(h_ref, w_ref, cw_ref, cb_ref, o_ref, s_ref, t_ref, *, ctx_rows, ctx_seg, lat_seg):
    tm, tn = o_ref.shape
    i = pl.program_id(0)
    half = CONV_K // 2
    cwid = 2 * LANES
    per = ctx_seg // lat_seg
    nv = lat_seg // 8
    is_lat = i * tm >= ctx_rows
    sub = lax.broadcasted_iota(jnp.int32, (8, LANES), 0)

    @pl.when((i == 0) & (pl.program_id(1) == 0))
    def _():
        for c in range(tn // LANES):
            s_ref[c, 0:HALO, :] = jnp.zeros((HALO, LANES), F32)
            s_ref[c, HALO + tm:2 * HALO + tm, :] = jnp.zeros((HALO, LANES), F32)

    def pad_select(v, crosses, static_boundary):
        if static_boundary:
            return jnp.where(crosses, 0.0, v)
        return jnp.where(jnp.logical_and(crosses, is_lat), 0.0, v)

    for nb in range(tn // cwid):
        acc = _dot(h_ref[...], w_ref[0, :, nb * cwid:(nb + 1) * cwid])
        for c2 in range(cwid // LANES):
            s_ref[nb * (cwid // LANES) + c2, HALO:HALO + tm, :] = acc[:, c2 * LANES:(c2 + 1) * LANES]
        for c2 in range(cwid // LANES):
            c = nb * (cwid // LANES) + c2
            lanes = slice(c * LANES, (c + 1) * LANES)
            wts = [cw_ref[k:k + 1, lanes] for k in range(CONV_K)]
            bias = cb_ref[:, lanes]
            for g in range(tm // lat_seg):
                r0 = HALO + g * lat_seg
                vec = {}
                for e in range(-half, nv + half):
                    v = s_ref[c, pl.ds(r0 + e, 8, stride=8), :]
                    if e < 0:
                        v = pad_select(v, sub == 0, g % per == 0)
                    if e >= nv:
                        v = pad_select(v, sub == 7, g % per == per - 1)
                    vec[e] = v
                for a in range(nv):
                    out = bias
                    for dlt in range(-half, half + 1):
                        out = out + vec[a + dlt] * wts[half + dlt]
                    t_ref[c, pl.ds(g * lat_seg + a, 8, stride=8), :] = _silu(out)
            o_ref[:, lanes] = t_ref[c].astype(BF16)


def _xbc_call(h, w_in, layer, col0, conv_w, conv_b, ctx_rows, ctx_seg):
    m, d = h.shape
    n = conv_w.shape[1]
    tm, tn = 512, 1024
    assert col0 % tn == 0 and n % tn == 0
    assert tm % ctx_seg == 0 and ctx_seg % GRID_W == 0 and ctx_rows % tm == 0
    assert GRID_W == 64 and CONV_K // 2 <= HALO
    kern = functools.partial(_xbc_kernel, ctx_rows=ctx_rows, ctx_seg=ctx_seg, lat_seg=GRID_W)
    stage_rows = tm + 2 * HALO
    return pl.pallas_call(
        kern,
        grid=(m // tm, n // tn),
        in_specs=[pl.BlockSpec((tm, d), lambda i, j: (i, 0)),
                  pl.BlockSpec((1, d, tn), lambda i, j: (layer, 0, col0 // tn + j)),
                  pl.BlockSpec((CONV_K, tn), lambda i, j: (0, j)),
                  pl.BlockSpec((1, tn), lambda i, j: (0, j))],
        out_specs=pl.BlockSpec((tm, tn), lambda i, j: (i, j)),
        out_shape=jax.ShapeDtypeStruct((m, n), BF16),
        scratch_shapes=[pltpu.VMEM((tn // LANES, stage_rows, LANES), F32),
                        pltpu.VMEM((tn // LANES, tm, LANES), F32)],
        compiler_params=_cparams(("arbitrary", "arbitrary")),
        name="inproj_xbc",
    )(h, w_in, conv_w, conv_b.reshape(1, n))


def _dt_kernel(h_ref, w_ref, b_ref, o_ref):
    v = _dot(h_ref[...], w_ref[0]) + b_ref[...]
    o_ref[...] = jnp.maximum(v, 0.0) + jnp.log1p(jnp.exp(-jnp.abs(v)))


def _dt_call(h, w_in, layer, col0, b):
    m, d = h.shape
    n = LANES
    assert col0 % n == 0
    tm = 1024
    return pl.pallas_call(
        _dt_kernel,
        grid=(m // tm,),
        in_specs=[pl.BlockSpec((tm, d), lambda i: (i, 0)),
                  pl.BlockSpec((1, d, n), lambda i: (layer, 0, col0 // n)),
                  pl.BlockSpec((1, n), lambda i: (0, 0))],
        out_specs=pl.BlockSpec((tm, n), lambda i: (i, 0)),
        out_shape=jax.ShapeDtypeStruct((m, n), F32),
        compiler_params=_cparams(("arbitrary",)),
        name="inproj_dt",
    )(h, w_in, b)


def _fnet_in_kernel(h_ref, w_ref, cs_ref, xc_ref, xs_ref):
    f = _dot(h_ref[...], w_ref[...]).astype(BF16)
    gw = cs_ref.shape[0]
    for g in range(FNET_GROUPS):
        sl = slice(g * gw, (g + 1) * gw)
        r = _dot(f[:, sl], cs_ref[...])
        xc_ref[:, sl] = r[:, :gw].astype(BF16)
        xs_ref[:, sl] = r[:, gw:].astype(BF16)


def _fnet_in_call(h, w, cs):
    m, d = h.shape
    n = w.shape[1]
    tm = 512
    spec_o = pl.BlockSpec((tm, n), lambda i: (i, 0))
    return pl.pallas_call(
        _fnet_in_kernel,
        grid=(m // tm,),
        in_specs=[pl.BlockSpec((tm, d), lambda i: (i, 0)),
                  pl.BlockSpec((d, n), lambda i: (0, 0)),
                  pl.BlockSpec(cs.shape, lambda i: (0, 0))],
        out_specs=[spec_o, spec_o],
        out_shape=[jax.ShapeDtypeStruct((m, n), BF16)] * 2,
        compiler_params=_cparams(("arbitrary",)),
        name="inproj_fnet",
    )(h, w, cs)


def _fnet_pos_kernel(xc_c_ref, xs_c_ref, xc_l_ref, xs_l_ref, cc_ref, sc_ref, cl_ref, sl_ref, o_ref,
                     *, n_ctx_tiles):
    i = pl.program_id(0)

    @pl.when(i < n_ctx_tiles)
    def _():
        o_ref[...] = (_dot(cc_ref[...], xc_c_ref[...]) - _dot(sc_ref[...], xs_c_ref[...])).astype(BF16)

    @pl.when(i >= n_ctx_tiles)
    def _():
        o_ref[...] = (_dot(cl_ref[...], xc_l_ref[...]) - _dot(sl_ref[...], xs_l_ref[...])).astype(BF16)


def _fnet_pos_call(xc, xs, cc, sc, cl, sl, ctx_rows):
    m, n = xc.shape
    tm = cc.shape[0]
    lat_len = cl.shape[0]
    n_ctx = ctx_rows // tm
    per_lat = lat_len // tm
    ctx_spec = pl.BlockSpec((tm, n), lambda i: (jnp.minimum(i, n_ctx - 1), 0))
    lat_spec = pl.BlockSpec(
        (lat_len, n), lambda i: (ctx_rows // lat_len + jnp.maximum(i - n_ctx, 0) // per_lat, 0))
    dft_c = pl.BlockSpec((tm, tm), lambda i: (0, 0))
    dft_l = pl.BlockSpec((tm, lat_len), lambda i: (jnp.maximum(i - n_ctx, 0) % per_lat, 0))
    kern = functools.partial(_fnet_pos_kernel, n_ctx_tiles=n_ctx)
    return pl.pallas_call(
        kern,
        grid=(m // tm,),
        in_specs=[ctx_spec, ctx_spec, lat_spec, lat_spec, dft_c, dft_c, dft_l, dft_l],
        out_specs=pl.BlockSpec((tm, n), lambda i: (i, 0)),
        out_shape=jax.ShapeDtypeStruct((m, n), BF16),
        compiler_params=_cparams(("arbitrary",)),
        name="fnet_pos",
    )(xc, xs, xc, xs, cc, sc, cl, sl)


def _sgu_kernel(h_ref, w_ref, g_ref, wsp_ref, bsp_ref, o_ref):
    tm, width = o_ref.shape
    hd = width // SGU_HEADS
    acc = _dot(h_ref[...], w_ref[...])
    k0 = math.sqrt(2.0 / math.pi)
    s = 0.5 * acc * (1.0 + jnp.tanh(k0 * (acc + 0.044715 * (acc * acc * acc))))
    u = s[:, :width]
    v = _rms(s[:, width:], g_ref[...]).astype(BF16)
    for k in range(tm // CHUNK):
        rows = slice(k * CHUNK, (k + 1) * CHUNK)
        for hh in range(SGU_HEADS):
            cols = slice(hh * hd, (hh + 1) * hd)
            sp = _dot(wsp_ref[hh], v[rows, cols]) + bsp_ref[:, hh:hh + 1]
            o_ref[rows, cols] = (u[rows, cols] * sp).astype(BF16)


def _sgu_call(h, w, g_sgu, w_sp, b_sp_t):
    m, d = h.shape
    n = w.shape[1]
    width = n // 2
    tm = 512
    return pl.pallas_call(
        _sgu_kernel,
        grid=(m // tm,),
        in_specs=[pl.BlockSpec((tm, d), lambda i: (i, 0)),
                  pl.BlockSpec((d, n), lambda i: (0, 0)),
                  pl.BlockSpec((1, width), lambda i: (0, 0)),
                  pl.BlockSpec(w_sp.shape, lambda i: (0, 0, 0)),
                  pl.BlockSpec(b_sp_t.shape, lambda i: (0, 0))],
        out_specs=pl.BlockSpec((tm, width), lambda i: (i, 0)),
        out_shape=jax.ShapeDtypeStruct((m, width), BF16),
        compiler_params=_cparams(("arbitrary",)),
        name="inproj_sgu",
    )(h, w, g_sgu.reshape(1, width), w_sp, b_sp_t)


def _ssd_kernel(tbl_ref, xs_ref, b_ref, c_ref, dt_ref, h0_ref, e3_ref, tri3_ref, alog_ref, *rest,
                reverse, final, n_ctx_seq, lane_off):
    if final:
        yprev_ref, zs_ref, dsk_ref, gssd_ref, y_ref, hfin_ref, st_ref, ybuf_ref = rest
    else:
        y_ref, hfin_ref, st_ref = rest
    s = pl.program_id(0)
    seq = tbl_ref[1, s]
    first = tbl_ref[2, s]
    last = tbl_ref[3, s]
    T = CHUNK
    hp = xs_ref.shape[1]
    gw = hp // SSD_GROUPS
    n_heads = hp // SSD_HEAD_DIM

    @pl.when(first == 1)
    def _():
        h0 = h0_ref[0, 0, 0].reshape(hp, SSD_STATE)
        h0 = jnp.where(seq < n_ctx_seq, 0.0, h0)
        st_ref[...] = h0.T

    row = lax.broadcasted_iota(jnp.int32, (T, T), 0)
    col = lax.broadcasted_iota(jnp.int32, (T, T), 1)
    tri = (row <= col) if reverse else (row >= col)

    dt = dt_ref[...]
    a = dt * (-jnp.exp(alog_ref[...]))
    acs = _dot(tri3_ref[...], jnp.concatenate(_split3(a), axis=0))
    end = acs[0:1, :] if reverse else acs[T - 1:T, :]
    lr_t = (acs - jnp.log(dt)).T
    w_t = (jnp.exp(end - acs) * dt).T
    ex = _dot(jnp.concatenate(_split3(jnp.exp(acs)), axis=1), e3_ref[...])
    dec = ex[0:1, :] if reverse else ex[T - 1:T, :]

    lane =lax.broadcasted_iota(jnp.int32, (1, LANES), 1)
    low_bf = jnp.where(lane < SSD_HEAD_DIM, 1.0, 0.0).astype(BF16)
    high_bf = jnp.where(lane < SSD_HEAD_DIM, 0.0, 1.0).astype(BF16)
    if final:
        dsum = dsk_ref[0:1, :] + dsk_ref[1:2, :]
        sq_acc = jnp.zeros((T, LANES), F32)

    heads_per_group = n_heads // SSD_GROUPS
    for g in range(SSD_GROUPS):
        bg = b_ref[:, g * SSD_STATE:(g + 1) * SSD_STATE]
        cg = c_ref[:, g * SSD_STATE:(g + 1) * SSD_STATE]
        gs = slice(g * gw, (g + 1) * gw)
        cb = lax.dot_general(cg, bg, (((1,), (1,)), ((), ())), preferred_element_type=F32)
        b_t = bg.astype(F32).T
        yoff = _dot(cg, st_ref[:, gs].astype(BF16))
        for q in range(heads_per_group // 2):
            ms, bs = [], []
            for r in (2 * q, 2 * q + 1):
                hl = lane_off + g * heads_per_group + r
                diff = acs[:, hl:hl + 1] - lr_t[hl:hl + 1, :]
                ms.append((cb * jnp.exp(jnp.where(tri, diff, -jnp.inf))).astype(BF16))
                bs.append((b_t * w_t[hl:hl + 1, :]).astype(BF16))
            c0 = g * gw + q * LANES
            cs_ = slice(c0, c0 + LANES)
            xpair = xs_ref[:, cs_]
            rhs = jnp.concatenate([xpair * low_bf, xpair * high_bf], axis=0)
            yd = _dot(jnp.concatenate(ms, axis=1), rhs)
            st_ref[:, cs_] = st_ref[:, cs_] * dec[:, cs_] + _dot(jnp.concatenate(bs, axis=1), rhs)
            yblk = yd + yoff[:, q * LANES:(q + 1) * LANES] * ex[:, cs_]
            if final:
                t = ((yblk + yprev_ref[:, cs_].astype(F32) + xpair.astype(F32) * dsum[:, cs_])
                     * zs_ref[:, cs_].astype(F32))
                sq_acc = sq_acc + t * t
                ybuf_ref[:, cs_] = t
            else:
                y_ref[:, cs_] = yblk.astype(BF16)

    if final:
        scale = lax.rsqrt(jnp.sum(sq_acc, axis=-1, keepdims=True) / hp + EPS)
        y_ref[...] = (ybuf_ref[...] * scale * gssd_ref[...]).astype(BF16)

    @pl.when(last == 1)
    def _():
        hfin_ref[0] = st_ref[...].T


def _ssd_tables(n_ctx_seq, ctx_chunks, n_lat_seq, lat_chunks, reverse):
    blk, seq, first, last = [], [], [], []
    base = 0
    for sidx, nch in [(i, ctx_chunks) for i in range(n_ctx_seq)] + \
                     [(n_ctx_seq + i, lat_chunks) for i in range(n_lat_seq)]:
        order = list(range(nch))[::-1] if reverse else list(range(nch))
        for k, cidx in enumerate(order):
            blk.append(base + cidx)
            seq.append(sidx)
            first.append(1 if k == 0 else 0)
            last.append(1 if k == nch - 1 else 0)
        base += nch
    return np.array([blk, seq, first, last], dtype=np.int32)


def _ssd_call(xbc, dt, state_ssd, layer, e3_mat, tri3, alog128, geom, *, reverse, final, extra=()):
    n_ctx_seq, ctx_chunks, n_lat_seq, lat_chunks = geom
    m = xbc.shape[0]
    hp = e3_mat.shape[1]
    n_groups_cols = SSD_GROUPS * SSD_STATE
    tbl = jnp.asarray(_ssd_tables(n_ctx_seq, ctx_chunks, n_lat_seq, lat_chunks, reverse))
    steps = tbl.shape[1]
    n_seq = n_ctx_seq + n_lat_seq
    direction = 1 if reverse else 0
    n_heads = hp // SSD_HEAD_DIM
    kern = functools.partial(_ssd_kernel, reverse=reverse, final=final, n_ctx_seq=n_ctx_seq,
                             lane_off=direction * n_heads)
    rowblk = lambda width, cblk: pl.BlockSpec((CHUNK, width), lambda s, t: (t[0, s], cblk))
    const2 = lambda shape: pl.BlockSpec(shape, lambda s, t: (0, 0))
    in_specs = [
        rowblk(hp, 0),
        rowblk(n_groups_cols, hp // n_groups_cols),
        rowblk(n_groups_cols, hp // n_groups_cols + 1),
        rowblk(LANES, 0),
        pl.BlockSpec((1, 1, 1) + state_ssd.shape[3:],
                     lambda s, t: (jnp.maximum(t[1, s] - n_ctx_seq, 0), layer, direction, 0, 0, 0)),
        const2(e3_mat.shape), const2(tri3.shape), const2(alog128.shape),
    ]
    args = [xbc, xbc, xbc, dt, state_ssd, e3_mat, tri3, alog128]
    scratch = [pltpu.VMEM((SSD_STATE, hp), F32)]
    if final:
        yprev, zs, dsk, gssd = extra
        in_specs += [rowblk(hp, 0), rowblk(hp, 0), const2(dsk.shape), const2(gssd.shape)]
        args += [yprev, zs, dsk, gssd]
        scratch.append(pltpu.VMEM((CHUNK, hp), F32))
    y_dtype = BF16
    grid_spec = pltpu.PrefetchScalarGridSpec(
        num_scalar_prefetch=1,
        grid=(steps,),
        in_specs=in_specs,
        out_specs=[rowblk(hp, 0),
                   pl.BlockSpec((1, hp, SSD_STATE), lambda s, t: (t[1, s], 0, 0))],
        scratch_shapes=scratch,
    )
    return pl.pallas_call(
        kern,
        grid_spec=grid_spec,
        out_shape=[jax.ShapeDtypeStruct((m, hp), y_dtype),
                   jax.ShapeDtypeStruct((n_seq, hp, SSD_STATE), F32)],
        compiler_params=_cparams(("arbitrary",)),
        name="ssd_bwd" if reverse else "ssd_fwd",
    )(tbl, *args)


def _mixout_kernel(ys_ref, yf_ref, yg_ref, x_ref, w_ref, mod_ref, ng_ref, xo_ref, ho_ref):
    ks = ys_ref.shape[1]
    kf = yf_ref.shape[1]
    tm = xo_ref.shape[0]
    n_sub = 2
    for r in range(n_sub):
        rows = slice(r * (tm // n_sub), (r + 1) * (tm // n_sub))
        mm = (_dot(ys_ref[rows, :], w_ref[:ks, :]) + _dot(yf_ref[rows, :], w_ref[ks:ks + kf, :])
              + _dot(yg_ref[rows, :], w_ref[ks + kf:, :]))
        xn = x_ref[rows, :] + mod_ref[0, 5:6, :] * _rms(mm, ng_ref[3:4, :])
        xo_ref[rows, :] = xn
        hn = _rms(xn, ng_ref[4:5, :]) * (1.0 + mod_ref[0, 7:8, :]) + mod_ref[0, 6:7, :]
        ho_ref[rows, :] = hn.astype(BF16)


def _mixout_call(yssd, yfnet, ysgu, x, w_out, layer, mod, ng, rows_per_group):
    m, d = x.shape
    assert yssd.shape[1] + yfnet.shape[1] + ysgu.shape[1] == w_out.shape[1]
    tm = 512
    row = lambda width: pl.BlockSpec((tm, width), lambda i: (i, 0))
    return pl.pallas_call(
        _mixout_kernel,
        grid=(m // tm,),
        in_specs=[row(yssd.shape[1]), row(yfnet.shape[1]), row(ysgu.shape[1]), row(d),
                  pl.BlockSpec((None,) + w_out.shape[1:], lambda i: (layer, 0, 0),
                               pipeline_mode=pl.Buffered(1)),
                  pl.BlockSpec((1, N_MOD, d), lambda i: (i * tm // rows_per_group, 0, 0)),
                  pl.BlockSpec(ng.shape, lambda i: (0, 0))],
        out_specs=[row(d), row(d)],
        out_shape=[jax.ShapeDtypeStruct((m, d), F32), jax.ShapeDtypeStruct((m, d), BF16)],
        compiler_params=_cparams(("arbitrary",)),
        name="mixout",
    )(yssd, yfnet, ysgu, x, w_out, mod, ng)


def _dft_tables(n):
    scale = 1.0 / math.sqrt(n)
    r = int(round(math.sqrt(n)))
    if r * r != n or n < 1024:
        j = lax.broadcasted_iota(jnp.int32, (n, n), 0)
        k = lax.broadcasted_iota(jnp.int32, (n, n), 1)
        ang = ((j * k) % n).astype(F32) * (2.0 * math.pi / n)
        return (jnp.cos(ang) * scale).astype(BF16), (jnp.sin(ang) * scale).astype(BF16)
    jj = lax.broadcasted_iota(jnp.int32, (r, n), 0)
    kk = lax.broadcasted_iota(jnp.int32, (r, n), 1)
    a1 = ((jj * r * kk) % n).astype(F32) * (2.0 * math.pi / n)
    a2 = ((jj * kk) % n).astype(F32) * (2.0 * math.pi / n)
    c1, s1 = (jnp.cos(a1) * scale)[:, None, :], (jnp.sin(a1) * scale)[:, None, :]
    c2, s2 = jnp.cos(a2)[None, :, :], jnp.sin(a2)[None, :, :]
    cos = (c1 * c2 - s1 * s2).reshape(n, n)
    sin = (s1 * c2 + c1 * s2).reshape(n, n)
    return cos.astype(BF16), sin.astype(BF16)


def kernel(x_prompt, x_sample, state_ssd, c, c_ctx, w_mod, b_mod, norm_g, w_ffn1_up, w_ffn1_down,
           w_in, conv_w, conv_b, a_log, dt_bias, d_skip, g_ssd, g_sgu, w_sp, b_sp, w_out,
           w_ffn2_up, w_ffn2_down):
    batch, seq, d = x_prompt.shape
    dec_batch, dec_seq, _ = x_sample.shape
    depth = w_mod.shape[0]
    ctx_rows = batch * seq
    assert ctx_rows == dec_seq, "row tiles map to modulation groups in units of dec_seq rows"
    n_heads = a_log.shape[2]
    hp = n_heads * SSD_HEAD_DIM
    gn = SSD_GROUPS * SSD_STATE
    off_dt = hp + hp + 2 * gn
    off_fnet = off_dt + 2 * n_heads
    fnet_w = w_out.shape[1] // 4
    off_sgu = off_fnet + fnet_w

    cvec = jnp.concatenate([c_ctx[None, :], c, jnp.zeros((8 - 1 - dec_batch, d), F32)], axis=0)
    mod_all = _mod_call(cvec, w_mod, b_mod).reshape(depth, 8, N_MOD, d)

    gw = fnet_w // FNET_GROUPS
    cc_g, sc_g = _dft_tables(gw)
    cs_chan = jnp.concatenate([cc_g, sc_g], axis=1)
    cc, sc = _dft_tables(seq)
    cl, sl = _dft_tables(dec_seq)
    e_np = np.zeros((2, LANES, hp), np.float32)
    for dr in range(2):
        for hh in range(n_heads):
            e_np[dr, dr * n_heads + hh, hh * SSD_HEAD_DIM:(hh + 1) * SSD_HEAD_DIM] = 1.0
    e3_mats = [jnp.asarray(np.tile(e_np[dr], (3, 1)), BF16) for dr in range(2)]
    tidx = np.arange(CHUNK)
    tri3s = [jnp.asarray(np.tile((tidx[:, None] >= tidx[None, :]).astype(np.float32), (1, 3)), BF16),
             jnp.asarray(np.tile((tidx[:, None] <= tidx[None, :]).astype(np.float32), (1, 3)), BF16)]
    geom = (batch, seq // CHUNK, dec_batch, dec_seq // CHUNK)

    x, h = _prenorm_call(x_prompt.reshape(ctx_rows, d), x_sample.reshape(dec_batch * dec_seq, d),
                         mod_all[0], norm_g[0], dec_seq)
    w_in_bf = w_in.astype(BF16)
    w1u, w1d = w_ffn1_up.astype(BF16), w_ffn1_down.astype(BF16)
    w2u, w2d = w_ffn2_up.astype(BF16), w_ffn2_down.astype(BF16)
    w_out_bf = w_out.astype(BF16)
    states = []
    for l in range(depth):
        mod = mod_all[l]
        ng = norm_g[l]
        w_f = w_in_bf[l, :, off_fnet:off_sgu]
        w_s = w_in_bf[l, :, off_sgu:]
        dtb = jnp.pad(dt_bias[l].reshape(1, 2 * n_heads), ((0, 0), (0, LANES - 2 * n_heads)))
        alog128 = jnp.pad(a_log[l].reshape(1, 2 * n_heads), ((0, 0), (0, LANES - 2 * n_heads)))
        dskx = jnp.repeat(d_skip[l], SSD_HEAD_DIM, axis=1)

        x, h = _ffn_call(h, x, w1u, w1d, l, mod, ng, mod, ng,
                         dec_seq, gate=2, ng_out=1, ng_next=2, sh_next=3, sc_next=4, emit_next=True)

        zs = _z_call(h, w_in_bf, l, 0, hp)
        xbc = _xbc_call(h, w_in_bf, l, hp, conv_w[l], conv_b[l], ctx_rows, seq)
        dt = _dt_call(h, w_in_bf, l, off_dt, dtb)
        xc, xsn = _fnet_in_call(h, w_f, cs_chan)
        ysgu = _sgu_call(h, w_s, g_sgu[l], w_sp[l].astype(BF16), b_sp[l].T)

        y_f, hfin_f = _ssd_call(xbc, dt, state_ssd, l, e3_mats[0], tri3s[0], alog128, geom,
                                reverse=False, final=False)
        yssd, hfin_b = _ssd_call(xbc, dt, state_ssd, l, e3_mats[1], tri3s[1], alog128, geom,
                                 reverse=True, final=True,
                                 extra=(y_f, zs, dskx, g_ssd[l].reshape(1, hp)))
        states.append(jnp.stack([hfin_f[:batch], hfin_b[:batch]], axis=1))

        yfnet = _fnet_pos_call(xc, xsn, cc, sc, cl, sl, ctx_rows)
        x, h = _mixout_call(yssd, yfnet, ysgu, x, w_out_bf, l, mod, ng, dec_seq)

        last = l == depth - 1
        nxt = l if last else l + 1
        res = _ffn_call(h, x, w2u, w2d, l, mod, ng,
                        mod_all[nxt], norm_g[nxt], dec_seq, gate=8, ng_out=5, ng_next=0,
                        sh_next=0, sc_next=1, emit_next=not last, split_rows=ctx_rows)
        if not last:
            x, h = res

    y_prompt = res[0].reshape(batch, seq, d)
    y_sample = res[1].reshape(dec_batch, dec_seq, d)
    new_state = jnp.stack(states, axis=1).reshape(
        batch, depth, 2, n_heads, SSD_HEAD_DIM, SSD_STATE).astype(x_prompt.dtype)
    return (y_prompt, y_sample, new_state)
```
